```python
import math
import jax
import jax.numpy as jnp
from jax import lax
import numpy as np

D_MODEL = 1024
BATCH = 2
SEQ = 8192
DEPTH = 2

ROPE_THETA = 10000.0
NORM_EPS = 1e-6
Q_BLOCK = 128

DIFF_HEADS = 4
DIFF_HEAD_DIM = 64
DIFF_QK_WIDTH = DIFF_HEADS * 2 * DIFF_HEAD_DIM
DIFF_WIDTH = DIFF_HEADS * 2 * DIFF_HEAD_DIM
DIFF_SUBLN_EPS = 1e-5

MLA_HEADS = 4
MLA_NOPE_DIM = 64
MLA_ROPE_DIM = 32
MLA_V_DIM = 64
MLA_Q_RANK = 256
MLA_KV_RANK = 128
MLA_WIDTH = MLA_HEADS * MLA_V_DIM

RET_HEADS = 4
RET_QK_DIM = 64
RET_V_DIM = 64
RET_CHUNK = 128
RET_QK_WIDTH = RET_HEADS * RET_QK_DIM
RET_WIDTH = RET_HEADS * RET_V_DIM
RET_GN_EPS = 1e-6

MIX_WIDTH = DIFF_WIDTH + MLA_WIDTH + RET_WIDTH

IN_SIZES = (DIFF_QK_WIDTH, DIFF_QK_WIDTH, DIFF_WIDTH,
            MLA_Q_RANK, MLA_KV_RANK, MLA_ROPE_DIM,
            RET_QK_WIDTH, RET_QK_WIDTH, RET_WIDTH, RET_WIDTH)
IN_COLS = sum(IN_SIZES)

FFN_HIDDEN = -(-8 * D_MODEL // (3 * 256)) * 256

kernel_name = "hybrid_diff_mla_retention_block"


def _rms_norm(t, gain, eps=NORM_EPS):
    t32 = t.astype(jnp.float32)
    t32 = t32 * lax.rsqrt(jnp.mean(t32 * t32, axis=-1, keepdims=True) + eps)
    return t32.astype(t.dtype) * gain


def _rope_tables(positions, dim, dtype):
    inv_freq = 1.0 / (ROPE_THETA ** (jnp.arange(0, dim, 2, dtype=jnp.float32) / dim))
    ang = positions.astype(jnp.float32)[:, :, None] * inv_freq
    return (jnp.cos(ang)[:, :, None, :].astype(dtype),
            jnp.sin(ang)[:, :, None, :].astype(dtype))


def _apply_rope(t, cos, sin):
    t1, t2 = jnp.split(t, 2, axis=-1)
    return jnp.concatenate([t1 * cos - t2 * sin, t2 * cos + t1 * sin], axis=-1)


def _split_cols(t):
    out = []
    start = 0
    for n in IN_SIZES:
        out.append(t[..., start:start + n])
        start += n
    return out


def _sweep_query_blocks(block_fn, *qs):
    b, s = qs[0].shape[:2]
    nb = s // Q_BLOCK

    def to_blocks(t):
        return jnp.moveaxis(t.reshape((b, nb, Q_BLOCK) + t.shape[2:]), 1, 0)

    def body(args):
        return block_fn(args[0] * Q_BLOCK, *args[1:])

    out = lax.map(body, (jnp.arange(nb), *[to_blocks(t) for t in qs]))
    out = jnp.moveaxis(out, 0, 1)
    return out.reshape((b, s) + out.shape[3:])


def _causal_mask(q_start, s):
    q_pos = q_start + jnp.arange(Q_BLOCK)
    return jnp.arange(s)[None, :] <= q_pos[:, None]


def _masked_softmax(scores, mask):
    return jax.nn.softmax(jnp.where(mask, scores, -jnp.inf), axis=-1)


def _diff_attention(q, k, v, lam_q1, lam_k1, lam_q2, lam_k2, subln, lam_init, cos, sin):
    b, s, _ = q.shape
    q = q.reshape(b, s, DIFF_HEADS, 2, DIFF_HEAD_DIM)
    k = k.reshape(b, s, DIFF_HEADS, 2, DIFF_HEAD_DIM)
    q1 = _apply_rope(q[..., 0, :], cos, sin)
    q2 = _apply_rope(q[..., 1, :], cos, sin)
    k1 = _apply_rope(k[..., 0, :], cos, sin)
    k2 = _apply_rope(k[..., 1, :], cos, sin)
    v = v.reshape(b, s, DIFF_HEADS, 2 * DIFF_HEAD_DIM)
    f32 = jnp.float32
    lam = (jnp.exp(jnp.sum(lam_q1.astype(f32) * lam_k1.astype(f32)))
           - jnp.exp(jnp.sum(lam_q2.astype(f32) * lam_k2.astype(f32))) + lam_init)
    scale = DIFF_HEAD_DIM ** -0.5

    def block(q_start, q1b, q2b):
        mask = _causal_mask(q_start, s)
        a1 = _masked_softmax(jnp.einsum('bqhd,bkhd->bhqk', q1b, k1).astype(f32) * scale, mask)
        a2 = _masked_softmax(jnp.einsum('bqhd,bkhd->bhqk', q2b, k2).astype(f32) * scale, mask)
        return jnp.einsum('bhqk,bkhe->bqhe', (a1 - lam * a2).astype(v.dtype), v)

    o = _sweep_query_blocks(block, q1, q2)
    o = _rms_norm(o, subln, DIFF_SUBLN_EPS) * (1.0 - lam_init)
    return o.reshape(b, s, DIFF_WIDTH)


def _mla(c_q, c_kv, k_rope, q_norm, w_uq, kv_norm, w_ukv, cos, sin):
    b, s, _ = c_q.shape
    q = (_rms_norm(c_q, q_norm) @ w_uq).reshape(b, s, MLA_HEADS, MLA_NOPE_DIM + MLA_ROPE_DIM)
    q_nope = q[..., :MLA_NOPE_DIM]
    q_rope = _apply_rope(q[..., MLA_NOPE_DIM:], cos, sin)
    kv = (_rms_norm(c_kv, kv_norm) @ w_ukv).reshape(b, s, MLA_HEADS, MLA_NOPE_DIM + MLA_V_DIM)
    k_nope = kv[..., :MLA_NOPE_DIM]
    v = kv[..., MLA_NOPE_DIM:]
    k_rope = _apply_rope(k_rope[:, :, None, :], cos, sin)[:, :, 0, :]
    scale = (MLA_NOPE_DIM + MLA_ROPE_DIM) ** -0.5

    def block(q_start, qn, qr):
        mask = _causal_mask(q_start, s)
        scores = (jnp.einsum('bqhd,bkhd->bhqk', qn, k_nope)
                  + jnp.einsum('bqhr,bkr->bhqk', qr, k_rope))
        p = _masked_softmax(scores.astype(jnp.float32) * scale, mask)
        return jnp.einsum('bhqk,bkhe->bqhe', p.astype(v.dtype), v)

    o = _sweep_query_blocks(block, q_nope, q_rope)
    return o.reshape(b, s, MLA_WIDTH)


def _retention(q, k, v, g, gn_w, gn_b, cos, sin):
    b, s, _ = q.shape
    h, dk, dv, c = RET_HEADS, RET_QK_DIM, RET_V_DIM, RET_CHUNK
    nc = s // c
    dt = q.dtype
    q = _apply_rope(q.reshape(b, s, h, dk), cos, sin)
    k = _apply_rope(k.reshape(b, s, h, dk), cos, sin) * (dk ** -0.5)
    v = v.reshape(b, s, h, dv)
    log_gamma = jnp.log1p(-jnp.exp2(-5.0 - jnp.arange(h, dtype=jnp.float32)))
    pos = jnp.arange(c, dtype=jnp.float32)
    rel = pos[:, None] - pos[None, :]
    decay = jnp.where(rel[None] >= 0,
                      jnp.exp(jnp.maximum(rel, 0.0)[None] * log_gamma[:, None, None]),
                      0.0).astype(dt)
    xi = jnp.exp((pos[:, None] + 1.0) * log_gamma[None, :]).astype(dt)
    zeta = jnp.exp((c - 1.0 - pos)[:, None] * log_gamma[None, :]).astype(dt)
    chunk_decay = jnp.exp(c * log_gamma).astype(dt)
    qc = q.reshape(b, nc, c, h, dk)
    kc = k.reshape(b, nc, c, h, dk)
    vc = v.reshape(b, nc, c, h, dv)
    inner = jnp.einsum('bnihd,bnjhd->bnhij', qc, kc) * decay
    inner = jnp.einsum('bnhij,bnjhe->bnihe', inner, vc)

    def step(state, kv_chunk):
        k_i, v_i = kv_chunk
        new_state = (jnp.einsum('bjhd,bjhe->bhde', k_i * zeta[None, :, :, None], v_i)
                     + chunk_decay[None, :, None, None] * state)
        return new_state, state

    state0 = jnp.zeros((b, h, dk, dv), dt)
    _, prev_states = lax.scan(step, state0, (jnp.moveaxis(kc, 1, 0), jnp.moveaxis(vc, 1, 0)))
    cross = jnp.einsum('bnihd,nbhde->bnihe', qc, prev_states) * xi[None, None, :, :, None]
    o = (inner + cross).reshape(b, s, h, dv)
    o32 = o.astype(jnp.float32)
    mu = jnp.mean(o32, axis=-1, keepdims=True)
    var = jnp.mean(jnp.square(o32 - mu), axis=-1, keepdims=True)
    o = ((o32 - mu) * lax.rsqrt(var + RET_GN_EPS)).astype(dt).reshape(b, s, RET_WIDTH)
    o = o * gn_w + gn_b
    return jax.nn.silu(g) * o


def setup_inputs(seed: int = 0) -> dict:
    key = jax.random.key(seed)
    ks = jax.random.split(key, 24)
    f32 = jnp.float32
    L = DEPTH

    def dense(k, shape, fan_in):
        return jax.random.normal(k, shape, f32) * (fan_in ** -0.5)

    def gain(k, shape):
        return 1.0 + 0.01 * jax.random.normal(k, shape, f32)

    x = jax.random.normal(ks[0], (BATCH, SEQ, D_MODEL), f32)
    positions = (jnp.arange(SEQ, dtype=jnp.int32)[None, :]
                 + jax.random.randint(ks[1], (BATCH, 1), 0, 1024, dtype=jnp.int32))
    return {
        "x": x,
        "positions": positions,
        "attn_norm": gain(ks[2], (L, D_MODEL)),
        "w_in": dense(ks[3], (L, D_MODEL, IN_COLS), D_MODEL),
        "diff_lam_q1": 0.1 * jax.random.normal(ks[4], (L, DIFF_HEAD_DIM), f32),
        "diff_lam_k1": 0.1 * jax.random.normal(ks[5], (L, DIFF_HEAD_DIM), f32),
        "diff_lam_q2": 0.1 * jax.random.normal(ks[6], (L, DIFF_HEAD_DIM), f32),
        "diff_lam_k2": 0.1 * jax.random.normal(ks[7], (L, DIFF_HEAD_DIM), f32),
        "diff_subln": gain(ks[8], (L, 2 * DIFF_HEAD_DIM)),
        "mla_q_norm": gain(ks[9], (L, MLA_Q_RANK)),
        "mla_w_uq": dense(ks[10], (L, MLA_Q_RANK, MLA_HEADS * (MLA_NOPE_DIM + MLA_ROPE_DIM)), MLA_Q_RANK),
        "mla_kv_norm": gain(ks[11], (L, MLA_KV_RANK)),
        "mla_w_ukv": dense(ks[12], (L, MLA_KV_RANK, MLA_HEADS * (MLA_NOPE_DIM + MLA_V_DIM)), MLA_KV_RANK),
        "ret_gn_w": gain(ks[13], (L, RET_WIDTH)),
        "ret_gn_b": 0.01 * jax.random.normal(ks[14], (L, RET_WIDTH), f32),
        "w_out": dense(ks[15], (L, MIX_WIDTH, D_MODEL), MIX_WIDTH),
        "ffn_norm": gain(ks[16], (L, D_MODEL)),
        "w_gate": dense(ks[17], (L, D_MODEL, FFN_HIDDEN), D_MODEL),
        "w_up": dense(ks[18], (L, D_MODEL, FFN_HIDDEN), D_MODEL),
        "w_down": dense(ks[19], (L, FFN_HIDDEN, D_MODEL), FFN_HIDDEN),
        "final_norm": gain(ks[20], (D_MODEL,)),
    }


def reference(x, positions, attn_norm, w_in, diff_lam_q1, diff_lam_k1, diff_lam_q2,
              diff_lam_k2, diff_subln, mla_q_norm, mla_w_uq, mla_kv_norm, mla_w_ukv,
              ret_gn_w, ret_gn_b, w_out, ffn_norm, w_gate, w_up, w_down, final_norm):
    dt = x.dtype
    cos_d, sin_d = _rope_tables(positions, DIFF_HEAD_DIM, dt)
    cos_m, sin_m = _rope_tables(positions, MLA_ROPE_DIM, dt)
    cos_r, sin_r = _rope_tables(positions, RET_QK_DIM, dt)
    for layer in range(DEPTH):
        h = _rms_norm(x, attn_norm[layer])
        (d_q, d_k, d_v, m_cq, m_ckv, m_krope,
         r_q, r_k, r_v, r_g) = _split_cols(h @ w_in[layer])
        lam_init = 0.8 - 0.6 * math.exp(-0.3 * layer)
        o_diff = _diff_attention(d_q, d_k, d_v, diff_lam_q1[layer], diff_lam_k1[layer],
                                 diff_lam_q2[layer], diff_lam_k2[layer], diff_subln[layer],
                                 lam_init, cos_d, sin_d)
        o_mla = _mla(m_cq, m_ckv, m_krope, mla_q_norm[layer], mla_w_uq[layer],
                     mla_kv_norm[layer], mla_w_ukv[layer], cos_m, sin_m)
        o_ret = _retention(r_q, r_k, r_v, r_g, ret_gn_w[layer], ret_gn_b[layer], cos_r, sin_r)
        mixed = jnp.concatenate([o_diff, o_mla, o_ret], axis=-1)
        x = x + mixed @ w_out[layer]
        h = _rms_norm(x, ffn_norm[layer])
        x = x + (jax.nn.silu(h @ w_gate[layer]) * (h @ w_up[layer])) @ w_down[layer]
    return _rms_norm(x, final_norm)
```

```python
import functools
import math

import numpy as np
import jax
import jax.numpy as jnp
from jax import lax
from jax.experimental import pallas as pl
from jax.experimental.pallas import tpu as pltpu

F32 = jnp.float32
BF16 = jnp.bfloat16

D_MODEL = 1024
DEPTH = 2
ROPE_THETA = 10000.0
NORM_EPS = 1e-6

DIFF_HEADS = 4
DIFF_HEAD_DIM = 64
DIFF_SUBLN_EPS = 1e-5
MLA_HEADS = 4
MLA_NOPE_DIM = 64
MLA_ROPE_DIM = 32
MLA_V_DIM = 64
MLA_Q_RANK = 256
MLA_KV_RANK = 128
RET_HEADS = 4
RET_QK_DIM = 64
RET_V_DIM = 64
RET_CHUNK = 128
RET_GN_EPS = 1e-6
FFN_HIDDEN = 2816

LANES = 128
HALF = LANES // 2
LOG2E = math.log2(math.e)

_OFF_DQ, _OFF_DK, _OFF_DV = 0, 512, 1024
_OFF_CQ, _OFF_CKV, _OFF_KR = 1536, 1792, 1920
_OFF_RQ, _OFF_RK, _OFF_RV, _OFF_RG = 1952, 2208, 2464, 2720
_IN_COLS = 2976

PRE_BM = 512
POST_BM = 512
ATT_BQ = 512
RET_ROWS = 512
FFN_CHUNK = 256
VMEM_LIMIT = 56 * 1024 * 1024


def _pair_perm():
    l = np.arange(LANES)
    return ((l % HALF) // 32) * 64 + (l // HALF) * 32 + (l % 32)


def _in_proj_columns():
    pp = _pair_perm()
    z = _IN_COLS
    cols = []
    for base in (_OFF_DQ, _OFF_DK):
        for h in range(DIFF_HEADS):
            cols.append(base + h * LANES + pp)
    cols.append(_OFF_DV + np.arange(512))
    cols.append(_OFF_CQ + np.arange(MLA_Q_RANK))
    cols.append(_OFF_CKV + np.arange(MLA_KV_RANK))
    kr = np.full(LANES, z)
    kr[32:48] = _OFF_KR + np.arange(16)
    kr[96:112] = _OFF_KR + 16 + np.arange(16)
    cols.append(kr)
    for base in (_OFF_RQ, _OFF_RK):
        for v in range(RET_HEADS // 2):
            cols.append(base + v * LANES + pp)
    cols.append(_OFF_RV + np.arange(256))
    cols.append(_OFF_RG + np.arange(256))
    return np.concatenate(cols).astype(np.int32)


_P_DQ, _P_DK, _P_DV = 0, 512, 1024
_P_CQ, _P_CKV, _P_KR = 1536, 1792, 1920
_P_RQ, _P_RK, _P_RV, _P_RG = 2048, 2304, 2560, 2816
_P_COLS = 3072


def _mla_head_lanes():
    qk_dim = MLA_NOPE_DIM + MLA_ROPE_DIM
    uq = np.full((MLA_HEADS, LANES), MLA_HEADS * qk_dim)
    ukv_k = np.full((MLA_HEADS, LANES), MLA_HEADS * (MLA_NOPE_DIM + MLA_V_DIM))
    ukv_v = np.full((MLA_HEADS, LANES), MLA_HEADS * (MLA_NOPE_DIM + MLA_V_DIM))
    for h in range(MLA_HEADS):
        uq[h, 0:32] = h * qk_dim + np.arange(32)
        uq[h, 32:48] = h * qk_dim + MLA_NOPE_DIM + np.arange(16)
        uq[h, 64:96] = h * qk_dim + 32 + np.arange(32)
        uq[h, 96:112] = h * qk_dim + MLA_NOPE_DIM + 16 + np.arange(16)
        kb = h * (MLA_NOPE_DIM + MLA_V_DIM)
        ukv_k[h, 0:32] = kb + np.arange(32)
        ukv_k[h, 64:96] = kb + 32 + np.arange(32)
        ukv_v[h, 0:64] = kb + MLA_NOPE_DIM + np.arange(64)
    return (uq.reshape(-1).astype(np.int32), ukv_k.reshape(-1).astype(np.int32),
            ukv_v.reshape(-1).astype(np.int32))


def _take_cols(w, idx):
    wz = jnp.concatenate([w, jnp.zeros((w.shape[0], 1), w.dtype)], axis=1)
    return jnp.take(wz, jnp.asarray(idx), axis=1)


def _rope_lane_params():
    l = np.arange(LANES)
    f64 = 1.0 / (ROPE_THETA ** (jnp.arange(0, 64, 2, dtype=F32) / 64))
    f32 = 1.0 / (ROPE_THETA ** (jnp.arange(0, 32, 2, dtype=F32) / 32))
    sign = np.where(l < HALF, -1.0, 1.0).astype(np.float32)
    inv_a = jnp.take(f64, jnp.asarray(l % 32))
    is_rope = ((l % HALF) >= 32) & ((l % HALF) < 48)
    inv_b = jnp.where(jnp.asarray(is_rope), jnp.take(f32, jnp.asarray(l % 16)), 0.0)
    invf = jnp.stack([inv_a, inv_b]).astype(F32)
    sgn = jnp.asarray(np.stack([sign, sign]))
    return invf, sgn


def _rope_table_kernel(pos_ref, invf_ref, sgn_ref, c64_ref, s64_ref, cm_ref, sm_ref):
    pos = pos_ref[...]
    a = pos * invf_ref[0:1, :]
    c64_ref[...] = jnp.cos(a)
    s64_ref[...] = jnp.sin(a) * sgn_ref[0:1, :]
    b = pos * invf_ref[1:2, :]
    cm_ref[...] = jnp.cos(b)
    sm_ref[...] = jnp.sin(b) * sgn_ref[1:2, :]


def _rope_tables(pos, invf, sgn):
    m = pos.shape[0]
    bm = 2048
    tab = jax.ShapeDtypeStruct((m, LANES), F32)
    row = pl.BlockSpec((bm, LANES), lambda i: (i, 0))
    return pl.pallas_call(
        _rope_table_kernel,
        grid=(m // bm,),
        in_specs=[pl.BlockSpec((bm, 1), lambda i: (i, 0)),
                  pl.BlockSpec((2, LANES), lambda i: (0, 0)),
                  pl.BlockSpec((2, LANES), lambda i: (0, 0))],
        out_specs=[row, row, row, row],
        out_shape=[tab, tab, tab, tab],
        name="rope_tables",
    )(pos, invf, sgn)


def _rms(t, gain, eps):
    return t * lax.rsqrt(jnp.mean(t * t, axis=-1, keepdims=True) + eps) * gain


def _rope_blocks(y, cos, sin):
    outs = []
    for j in range(y.shape[1] // LANES):
        blk = y[:, j * LANES:(j + 1) * LANES]
        outs.append(blk * cos + pltpu.roll(blk, HALF, 1) * sin)
    return outs[0] if len(outs) == 1 else jnp.concatenate(outs, axis=1)


def _pre_kernel(x_ref, g_ref, w_ref, c64_ref, s64_ref, cm_ref, sm_ref,
                qn_ref, wuq_ref, kvn_ref, wukv_ref,
                dq_ref, dk_ref, dv_ref, mq_ref, mk_ref, mv_ref,
                rq_ref, rk_ref, rv_ref, rg_ref, *, diff_qscale, mla_qscale, ret_kscale):
    hb = _rms(x_ref[...], g_ref[...], NORM_EPS).astype(BF16)

    def proj(off, n):
        return jnp.dot(hb, w_ref[:, off:off + n], preferred_element_type=F32)

    c64, s64 = c64_ref[...], s64_ref[...]
    cm, sm = cm_ref[...], sm_ref[...]

    dq_ref[...] = (_rope_blocks(proj(_P_DQ, 512), c64, s64) * diff_qscale).astype(BF16)
    dk_ref[...] = _rope_blocks(proj(_P_DK, 512), c64, s64).astype(BF16)
    dv_ref[...] = proj(_P_DV, 512).astype(BF16)

    cq = _rms(proj(_P_CQ, MLA_Q_RANK), qn_ref[...], NORM_EPS).astype(BF16)
    q = jnp.dot(cq, wuq_ref[...], preferred_element_type=F32)
    mq_ref[...] = (_rope_blocks(q, cm, sm) * mla_qscale).astype(BF16)
    ckv = _rms(proj(_P_CKV, MLA_KV_RANK), kvn_ref[...], NORM_EPS).astype(BF16)
    kv = jnp.dot(ckv, wukv_ref[...], preferred_element_type=F32)
    kr = _rope_blocks(proj(_P_KR, LANES), cm, sm)
    mk_ref[...] = (kv[:, :512] + jnp.concatenate([kr] * MLA_HEADS, axis=1)).astype(BF16)
    mv_ref[...] = kv[:, 512:].astype(BF16)

    rq_ref[...] = _rope_blocks(proj(_P_RQ, 256), c64, s64).astype(BF16)
    rk_ref[...] = (_rope_blocks(proj(_P_RK, 256), c64, s64) * ret_kscale).astype(BF16)
    rv_ref[...] = proj(_P_RV, 256).astype(BF16)
    rg_ref[...] = proj(_P_RG, 256)


def _const_spec(shape):
    return pl.BlockSpec(shape, lambda i: (0,) * len(shape), pipeline_mode=pl.Buffered(1))


def _pre_attention(x, gain, w_all, tabs, qn, wuq, kvn, wukv):
    m = x.shape[0]
    bm = PRE_BM
    row = lambda n: pl.BlockSpec((bm, n), lambda i: (i, 0))
    out = lambda n, dt=BF16: jax.ShapeDtypeStruct((m, n), dt)
    kern = functools.partial(
        _pre_kernel,
        diff_qscale=DIFF_HEAD_DIM ** -0.5 * LOG2E,
        mla_qscale=(MLA_NOPE_DIM + MLA_ROPE_DIM) ** -0.5 * LOG2E,
        ret_kscale=RET_QK_DIM ** -0.5)
    return pl.pallas_call(
        kern,
        grid=(m // bm,),
        in_specs=[row(D_MODEL), _const_spec((1, D_MODEL)), _const_spec((D_MODEL, _P_COLS)),
                  row(LANES), row(LANES), row(LANES), row(LANES),
                  _const_spec((1, MLA_Q_RANK)), _const_spec((MLA_Q_RANK, 512)),
                  _const_spec((1, MLA_KV_RANK)), _const_spec((MLA_KV_RANK, 1024))],
        out_specs=[row(512), row(512), row(512), row(512), row(512), row(512),
                   row(256), row(256), row(256), row(256)],
        out_shape=[out(512), out(512), out(512), out(512), out(512), out(512),
                   out(256), out(256), out(256), out(256, F32)],
        compiler_params=pltpu.CompilerParams(
            dimension_semantics=("parallel",), vmem_limit_bytes=VMEM_LIMIT),
        name="pre_attention",
    )(x, gain, w_all, *tabs, qn, wuq, kvn, wukv)


def _flash_kernel(*refs, n_maps, bq, lam_init):
    if n_maps == 2:
        q_ref, k_ref, v_ref, lam_ref, subln_ref, o_ref, m_sc, l_sc, acc_sc = refs
    else:
        q_ref, k_ref, v_ref, o_ref, m_sc, l_sc, acc_sc = refs
    i = pl.program_id(2)
    q = q_ref[...]
    if n_maps == 2:
        lane = lax.broadcasted_iota(jnp.int32, (1, LANES), 1)
        qs = [jnp.where(((lane % HALF) // 32) == mi, q, jnp.zeros_like(q)) for mi in range(2)]
    else:
        qs = [q]

    m_sc[...] = jnp.full(m_sc.shape, -jnp.inf, F32)
    l_sc[...] = jnp.zeros(l_sc.shape, F32)
    acc_sc[...] = jnp.zeros(acc_sc.shape, F32)

    def step(j, masked):
        start = pl.multiple_of(j * bq, bq)
        kj = k_ref[pl.ds(start, bq), :]
        vj = v_ref[pl.ds(start, bq), :]
        for mi in range(n_maps):
            s = lax.dot_general(qs[mi], kj, (((1,), (1,)), ((), ())),
                                preferred_element_type=F32)
            if masked:
                r = lax.broadcasted_iota(jnp.int32, (bq, bq), 0)
                c = lax.broadcasted_iota(jnp.int32, (bq, bq), 1)
                s = jnp.where(c <= r, s, -jnp.inf)
            m_prev = m_sc[mi]
            m_new = jnp.maximum(m_prev, jnp.max(s, axis=1, keepdims=True))
            alpha = jnp.exp2(m_prev - m_new)
            p = jnp.exp2(s - m_new)
            l_sc[mi] = alpha * l_sc[mi] + jnp.sum(p, axis=1, keepdims=True)
            acc_sc[mi] = alpha * acc_sc[mi] + jnp.dot(p.astype(BF16), vj,
                                                      preferred_element_type=F32)
            m_sc[mi] = m_new

    def body(j, carry):
        step(j, False)
        return carry

    lax.fori_loop(0, i, body, 0)
    step(i, True)

    if n_maps == 2:
        lv = lam_ref[...]
        lam = (jnp.exp(jnp.sum(lv[0:1] * lv[1:2], axis=1, keepdims=True))
               - jnp.exp(jnp.sum(lv[2:3] * lv[3:4], axis=1, keepdims=True)) + lam_init)
        o = acc_sc[0] / l_sc[0] - lam * (acc_sc[1] / l_sc[1])
        o = _rms(o, subln_ref[...], DIFF_SUBLN_EPS) * (1.0 - lam_init)
    else:
        o = acc_sc[0] / l_sc[0]
    o_ref[...] = o.astype(o_ref.dtype)


def _flash_attention(q, k, v, n_maps, lam=None, subln=None, lam_init=0.0):
    b, s, hw = q.shape
    heads = hw // LANES
    bq = ATT_BQ
    qspec = pl.BlockSpec((None, bq, LANES), lambda bi, h, i: (bi, i, h))
    kvspec = pl.BlockSpec((None, s, LANES), lambda bi, h, i: (bi, 0, h))
    in_specs = [qspec, kvspec, kvspec]
    args = [q, k, v]
    if n_maps == 2:
        in_specs += [pl.BlockSpec((4, DIFF_HEAD_DIM), lambda bi, h, i: (0, 0)),
                     pl.BlockSpec((1, LANES), lambda bi, h, i: (0, 0))]
        args += [lam, subln]
    return pl.pallas_call(
        functools.partial(_flash_kernel, n_maps=n_maps, bq=bq, lam_init=lam_init),
        grid=(b, heads, s // bq),
        in_specs=in_specs,
        out_specs=qspec,
        out_shape=jax.ShapeDtypeStruct((b, s, hw), BF16),
        scratch_shapes=[pltpu.VMEM((n_maps, bq, 1), F32),
                        pltpu.VMEM((n_maps, bq, 1), F32),
                        pltpu.VMEM((n_maps, bq, LANES), F32)],
        compiler_params=pltpu.CompilerParams(
            dimension_semantics=("parallel", "parallel", "arbitrary"),
            vmem_limit_bytes=VMEM_LIMIT),
        name="diff_attention" if n_maps == 2 else "mla_attention",
    )(*args)


def _retention_consts():
    h, c = RET_HEADS, RET_CHUNK
    log_gamma = jnp.log1p(-jnp.exp2(-5.0 - jnp.arange(h, dtype=F32)))
    pos = jnp.arange(c, dtype=F32)
    rel = pos[:, None] - pos[None, :]
    decay = jnp.where(rel[None] >= 0,
                      jnp.exp(jnp.maximum(rel, 0.0)[None] * log_gamma[:, None, None]), 0.0)
    xi = jnp.exp((pos[None, :] + 1.0) * log_gamma[:, None])
    zeta = jnp.exp((c - 1.0 - pos)[None, :] * log_gamma[:, None])
    cdec = jnp.exp(c * log_gamma)
    col = jnp.stack([xi, zeta, jnp.broadcast_to(cdec[:, None], (h, c))], axis=-1)
    return decay.astype(F32), col.astype(F32)


def _retention_kernel(q_ref, k_ref, v_ref, g_ref, dec_ref, col_ref, gw_ref, gb_ref,
                      o_ref, st_sc, *, rows):
    @pl.when(pl.program_id(1) == 0)
    def _():
        st_sc[...] = jnp.zeros(st_sc.shape, F32)

    lane = lax.broadcasted_iota(jnp.int32, (1, LANES), 1)
    low = lane < HALF
    pair_sel = [((lane % HALF) // 32) == hh for hh in range(2)]
    c = RET_CHUNK
    for ci in range(rows // c):
        r0 = ci * c
        for pv in range(RET_HEADS // 2):
            lanes = slice(pv * LANES, (pv + 1) * LANES)
            qp = q_ref[r0:r0 + c, lanes]
            kp = k_ref[r0:r0 + c, lanes]
            vp = v_ref[r0:r0 + c, lanes]
            outs = []
            for hh in range(2):
                h = 2 * pv + hh
                qh = jnp.where(pair_sel[hh], qp, jnp.zeros_like(qp))
                kh = jnp.where(pair_sel[hh], kp, jnp.zeros_like(kp))
                cols = col_ref[h]
                xi, zeta, cdec = cols[:, 0:1], cols[:, 1:2], cols[0:1, 2:3]
                s = lax.dot_general(qh, kh, (((1,), (1,)), ((), ())),
                                    preferred_element_type=F32)
                inner = jnp.dot((s * dec_ref[h]).astype(BF16), vp, preferred_element_type=F32)
                state = st_sc[h]
                cross = jnp.dot(qh, state.astype(BF16), preferred_element_type=F32) * xi
                kz = (kh.astype(F32) * zeta).astype(BF16)
                st_sc[h] = (lax.dot_general(kz, vp, (((0,), (0,)), ((), ())),
                                            preferred_element_type=F32) + cdec * state)
                outs.append(inner + cross)
            o = jnp.where(low, outs[0], outs[1])
            inv_n = 1.0 / RET_V_DIM
            s_lo = jnp.sum(jnp.where(low, o, 0.0), axis=1, keepdims=True)
            s_hi = jnp.sum(jnp.where(low, 0.0, o), axis=1, keepdims=True)
            mu = jnp.where(low, s_lo, s_hi) * inv_n
            d = o - mu
            d2 = d * d
            v_lo = jnp.sum(jnp.where(low, d2, 0.0), axis=1, keepdims=True)
            v_hi = jnp.sum(jnp.where(low, 0.0, d2), axis=1, keepdims=True)
            var = jnp.where(low, v_lo, v_hi) * inv_n
            on = d * lax.rsqrt(var + RET_GN_EPS) * gw_ref[:, lanes] + gb_ref[:, lanes]
            g = g_ref[r0:r0 + c, lanes]
            o_ref[r0:r0 + c, lanes] = (g * jax.nn.sigmoid(g) * on).astype(o_ref.dtype)


def _retention(q, k, v, g, decay, col, gw, gb):
    b, s, w = q.shape
    rows = RET_ROWS
    blk = pl.BlockSpec((None, rows, w), lambda bi, i: (bi, i, 0))
    c = RET_CHUNK
    return pl.pallas_call(
        functools.partial(_retention_kernel, rows=rows),
        grid=(b, s // rows),
        in_specs=[blk, blk, blk, blk,
                  pl.BlockSpec((RET_HEADS, c, c), lambda bi, i: (0, 0, 0)),
                  pl.BlockSpec((RET_HEADS, c, 3), lambda bi, i: (0, 0, 0)),
                  pl.BlockSpec((1, w), lambda bi, i: (0, 0)),
                  pl.BlockSpec((1, w), lambda bi, i: (0, 0))],
        out_specs=blk,
        out_shape=jax.ShapeDtypeStruct((b, s, w), BF16),
        scratch_shapes=[pltpu.VMEM((RET_HEADS, LANES, LANES), F32)],
        compiler_params=pltpu.CompilerParams(
            dimension_semantics=("parallel", "arbitrary")),
        name="retention",
    )(q, k, v, g, decay, col, gw, gb)


def _post_kernel(x_ref, od_ref, om_ref, or_ref, wod_ref, wom_ref, wor_ref, g_ref,
                 wg_ref, wu_ref, wd_ref, fg_ref, o_ref, a_sc, *, final):
    x1 = (x_ref[...]
          + jnp.dot(od_ref[...], wod_ref[...], preferred_element_type=F32)
          + jnp.dot(om_ref[...], wom_ref[...], preferred_element_type=F32)
          + jnp.dot(or_ref[...], wor_ref[...], preferred_element_type=F32))
    hb = _rms(x1, g_ref[...], NORM_EPS).astype(BF16)
    for c0 in range(0, FFN_HIDDEN, FFN_CHUNK):
        gate = jnp.dot(hb, wg_ref[:, c0:c0 + FFN_CHUNK], preferred_element_type=F32)
        up = jnp.dot(hb, wu_ref[:, c0:c0 + FFN_CHUNK], preferred_element_type=F32)
        a_sc[:, c0:c0 + FFN_CHUNK] = (gate * jax.nn.sigmoid(gate) * up).astype(BF16)
    y = x1 + jnp.dot(a_sc[...], wd_ref[...], preferred_element_type=F32)
    if final:
        y = _rms(y, fg_ref[...], NORM_EPS)
    o_ref[...] = y


def _post_attention(x, od, om, orr, wod, wom, wor, gain, wg, wu, wd, fgain, final):
    m = x.shape[0]
    bm = POST_BM
    row = lambda n: pl.BlockSpec((bm, n), lambda i: (i, 0))
    return pl.pallas_call(
        functools.partial(_post_kernel, final=final),
        grid=(m // bm,),
        in_specs=[row(D_MODEL), row(512), row(512), row(256),
                  _const_spec((512, D_MODEL)), _const_spec((512, D_MODEL)),
                  _const_spec((256, D_MODEL)), _const_spec((1, D_MODEL)),
                  _const_spec((D_MODEL, FFN_HIDDEN)), _const_spec((D_MODEL, FFN_HIDDEN)),
                  _const_spec((FFN_HIDDEN, D_MODEL)), _const_spec((1, D_MODEL))],
        out_specs=row(D_MODEL),
        out_shape=jax.ShapeDtypeStruct((m, D_MODEL), F32),
        scratch_shapes=[pltpu.VMEM((bm, FFN_HIDDEN), BF16)],
        compiler_params=pltpu.CompilerParams(
            dimension_semantics=("parallel",), vmem_limit_bytes=VMEM_LIMIT),
        name="post_attention",
    )(x, od, om, orr, wod, wom, wor, gain, wg, wu, wd, fgain)


def kernel(x, positions, attn_norm, w_in, diff_lam_q1, diff_lam_k1, diff_lam_q2, diff_lam_k2, diff_subln, mla_q_norm, mla_w_uq, mla_kv_norm, mla_w_ukv, ret_gn_w, ret_gn_b, w_out, ffn_norm, w_gate, w_up, w_down, final_norm):
    b, s, d = x.shape
    m = b * s
    in_cols = _in_proj_columns()
    uq_cols, ukvk_cols, ukvv_cols = _mla_head_lanes()
    invf, sgn = _rope_lane_params()
    tabs = _rope_tables(positions.astype(F32).reshape(m, 1), invf, sgn)
    decay, col = _retention_consts()

    mla_rows = np.full((MLA_HEADS, LANES), MLA_HEADS * MLA_V_DIM)
    for h in range(MLA_HEADS):
        mla_rows[h, :MLA_V_DIM] = h * MLA_V_DIM + np.arange(MLA_V_DIM)
    mla_rows = mla_rows.reshape(-1).astype(np.int32)

    xf = x.reshape(m, d)
    for layer in range(DEPTH):
        lam_init = 0.8 - 0.6 * math.exp(-0.3 * layer)
        w_all = _take_cols(w_in[layer], in_cols).astype(BF16)
        wuq = _take_cols(mla_w_uq[layer], uq_cols).astype(BF16)
        wukv = jnp.concatenate([_take_cols(mla_w_ukv[layer], ukvk_cols),
                                _take_cols(mla_w_ukv[layer], ukvv_cols)], axis=1).astype(BF16)
        (dq, dk, dv, mq, mk, mv, rq, rk, rv, rg) = _pre_attention(
            xf, attn_norm[layer][None], w_all, tabs,
            mla_q_norm[layer][None], wuq, mla_kv_norm[layer][None], wukv)

        sh = lambda t: t.reshape(b, s, t.shape[-1])
        lam = jnp.stack([diff_lam_q1[layer], diff_lam_k1[layer],
                         diff_lam_q2[layer], diff_lam_k2[layer]])
        o_diff = _flash_attention(sh(dq), sh(dk), sh(dv), 2, lam=lam,
                                  subln=diff_subln[layer][None], lam_init=lam_init)
        o_mla = _flash_attention(sh(mq), sh(mk), sh(mv), 1)
        o_ret = _retention(sh(rq), sh(rk), sh(rv), sh(rg), decay, col,
                           ret_gn_w[layer][None], ret_gn_b[layer][None])

        wo = w_out[layer]
        wod = wo[:512].astype(BF16)
        wo_m = jnp.concatenate([wo[512:768], jnp.zeros((1, d), wo.dtype)], axis=0)
        wom = jnp.take(wo_m, jnp.asarray(mla_rows), axis=0).astype(BF16)
        wor = wo[768:].astype(BF16)
        xf = _post_attention(
            xf, o_diff.reshape(m, 512), o_mla.reshape(m, 512), o_ret.reshape(m, 256),
            wod, wom, wor, ffn_norm[layer][None],
            w_gate[layer].astype(BF16), w_up[layer].astype(BF16), w_down[layer].astype(BF16),
            final_norm[None], final=(layer == DEPTH - 1))
    return xf.reshape(b, s, d)
```

```python
import functools
import math

import jax
import jax.numpy as jnp
from jax import lax
from jax.experimental import pallas as pl
from jax.experimental.pallas import tpu as pltpu

F32 = jnp.float32
BF16 = jnp.bfloat16

D_MODEL = 1024
DEPTH = 2
ROPE_THETA = 10000.0
NORM_EPS = 1e-6

DIFF_HEADS = 4
DIFF_HEAD_DIM = 64
DIFF_SUBLN_EPS = 1e-5
MLA_HEADS = 4
MLA_NOPE_DIM = 64
MLA_ROPE_DIM = 32
MLA_V_DIM = 64
MLA_Q_RANK = 256
MLA_KV_RANK = 128
RET_HEADS = 4
RET_QK_DIM = 64
RET_V_DIM = 64
RET_CHUNK = 128
RET_GN_EPS = 1e-6
FFN_HIDDEN = 2816

LANES = 128
HALF = LANES // 2
LOG2E = math.log2(math.e)

_OFF_DQ, _OFF_DK, _OFF_DV = 0, 512, 1024
_OFF_CQ, _OFF_CKV, _OFF_KR = 1536, 1792, 1920
_OFF_RQ, _OFF_RK, _OFF_RV, _OFF_RG = 1952, 2208, 2464, 2720

ATT_BLOCK = 512
ATT_SUB = 256
PRE_BM = ATT_BLOCK
POST_BM = 512
RET_ROWS = 512
FFN_CHUNK = 256
VMEM_LIMIT = 56 * 1024 * 1024

_NT = (((1,), (1,)), ((), ()))


def _pair_layout(w):
    k, n = w.shape
    return w.reshape(k, n // LANES, 2, 2, 32).transpose(0, 1, 3, 2, 4).reshape(k, n)


def _mla_q_layout(w):
    k = w.shape[0]
    w = w.reshape(k, MLA_HEADS, MLA_NOPE_DIM + MLA_ROPE_DIM)
    nope = w[..., :MLA_NOPE_DIM].reshape(k, MLA_HEADS, 2, 32)
    rope = w[..., MLA_NOPE_DIM:].reshape(k, MLA_HEADS, 2, 16)
    pad = jnp.zeros((k, MLA_HEADS, 2, 16), w.dtype)
    return jnp.concatenate([nope, rope, pad], axis=-1).reshape(k, MLA_HEADS * LANES)


def _mla_kv_layout(w):
    k = w.shape[0]
    w = w.reshape(k, MLA_HEADS, MLA_NOPE_DIM + MLA_V_DIM)
    nope = w[..., :MLA_NOPE_DIM].reshape(k, MLA_HEADS, 2, 32)
    pad = jnp.zeros((k, MLA_HEADS, 2, 32), w.dtype)
    wk = jnp.concatenate([nope, pad], axis=-1).reshape(k, MLA_HEADS * LANES)
    wv = w[..., MLA_NOPE_DIM:].reshape(k, MLA_HEADS * MLA_V_DIM)
    return wk, wv


def _mla_krope_layout(w):
    k = w.shape[0]
    kr = w.reshape(k, 2, 16)
    return jnp.concatenate([jnp.zeros((k, 2, 32), w.dtype), kr,
                            jnp.zeros((k, 2, 16), w.dtype)], axis=-1).reshape(k, LANES)


def _in_proj_layout(w):
    sl = lambda off, n: w[:, off:off + n]
    return jnp.concatenate([
        _pair_layout(sl(_OFF_DQ, 512)), _pair_layout(sl(_OFF_DK, 512)),
        sl(_OFF_CQ, MLA_Q_RANK), sl(_OFF_CKV, MLA_KV_RANK),
        _mla_krope_layout(sl(_OFF_KR, MLA_ROPE_DIM)),
        _pair_layout(sl(_OFF_RQ, 256)), _pair_layout(sl(_OFF_RK, 256)),
        sl(_OFF_RV, 256), sl(_OFF_RG, 256)], axis=1)


_P_DQ, _P_DK = 0, 512
_P_CQ, _P_CKV, _P_KR = 1024, 1280, 1408
_P_RQ, _P_RK, _P_RV, _P_RG = 1536, 1792, 2048, 2304
_P_COLS = 2560


def _rope_lane_params():
    lane = jnp.arange(LANES)
    f64 = 1.0 / (ROPE_THETA ** (jnp.arange(0, 64, 2, dtype=F32) / 64))
    f32 = 1.0 / (ROPE_THETA ** (jnp.arange(0, 32, 2, dtype=F32) / 32))
    sign = jnp.where(lane < HALF, -1.0, 1.0).astype(F32)
    inv_a = jnp.take(f64, lane % 32)
    is_rope = ((lane % HALF) >= 32) & ((lane % HALF) < 48)
    inv_b = jnp.where(is_rope, jnp.take(f32, lane % 16), 0.0)
    return jnp.stack([inv_a, inv_b]).astype(F32), jnp.stack([sign, sign])


def _rope_table_kernel(pos_ref, invf_ref, sgn_ref, c64_ref, s64_ref, cm_ref, sm_ref):
    pos = pos_ref[...]
    a = pos * invf_ref[0:1, :]
    c64_ref[...] = jnp.cos(a)
    s64_ref[...] = jnp.sin(a) * sgn_ref[0:1, :]
    b = pos * invf_ref[1:2, :]
    cm_ref[...] = jnp.cos(b)
    sm_ref[...] = jnp.sin(b) * sgn_ref[1:2, :]


def _rope_tables(pos, invf, sgn):
    m = pos.shape[0]
    bm = 2048
    tab = jax.ShapeDtypeStruct((m, LANES), F32)
    row = pl.BlockSpec((bm, LANES), lambda i: (i, 0))
    return pl.pallas_call(
        _rope_table_kernel,
        grid=(m // bm,),
        in_specs=[pl.BlockSpec((bm, 1), lambda i: (i, 0)),
                  pl.BlockSpec((2, LANES), lambda i: (0, 0)),
                  pl.BlockSpec((2, LANES), lambda i: (0, 0))],
        out_specs=[row, row, row, row],
        out_shape=[tab, tab, tab, tab],
        name="rope_tables",
    )(pos, invf, sgn)


def _rms(t, gain, eps):
    return t * lax.rsqrt(jnp.mean(t * t, axis=-1, keepdims=True) + eps) * gain


def _rope_blocks(y, cos, sin):
    outs = []
    for j in range(y.shape[1] // LANES):
        blk = y[:, j * LANES:(j + 1) * LANES]
        outs.append(blk * cos + pltpu.roll(blk, HALF, 1) * sin)
    return outs[0] if len(outs) == 1 else jnp.concatenate(outs, axis=1)


def _pre_kernel(x_ref, g_ref, w_ref, wdvt_ref, c64_ref, s64_ref, cm_ref, sm_ref,
                qn_ref, wuq_ref, kvn_ref, wuk_ref, wuvt_ref,
                dq_ref, dk_ref, dvt_ref, mq_ref, mk_ref, mvt_ref,
                rq_ref, rk_ref, rv_ref, rg_ref, *, diff_qscale, mla_qscale, ret_kscale):
    hb = _rms(x_ref[...], g_ref[...], NORM_EPS).astype(BF16)

    def proj(off, n):
        return jnp.dot(hb, w_ref[:, off:off + n], preferred_element_type=F32)

    c64, s64 = c64_ref[...], s64_ref[...]
    cm, sm = cm_ref[...], sm_ref[...]

    dq_ref[...] = (_rope_blocks(proj(_P_DQ, 512), c64, s64) * diff_qscale).astype(BF16)
    dk_ref[...] = _rope_blocks(proj(_P_DK, 512), c64, s64).astype(BF16)
    dvt_ref[...] = lax.dot_general(wdvt_ref[...], hb, _NT,
                                   preferred_element_type=F32).astype(BF16)

    cq = _rms(proj(_P_CQ, MLA_Q_RANK), qn_ref[...], NORM_EPS).astype(BF16)
    q = jnp.dot(cq, wuq_ref[...], preferred_element_type=F32)
    mq_ref[...] = (_rope_blocks(q, cm, sm) * mla_qscale).astype(BF16)
    ckv = _rms(proj(_P_CKV, MLA_KV_RANK), kvn_ref[...], NORM_EPS).astype(BF16)
    kn = jnp.dot(ckv, wuk_ref[...], preferred_element_type=F32)
    kr = _rope_blocks(proj(_P_KR, LANES), cm, sm)
    mk_ref[...] = (kn + jnp.concatenate([kr] * MLA_HEADS, axis=1)).astype(BF16)
    mvt_ref[...] = lax.dot_general(wuvt_ref[...], ckv, _NT,
                                   preferred_element_type=F32).astype(BF16)

    rq_ref[...] = _rope_blocks(proj(_P_RQ, 256), c64, s64).astype(BF16)
    rk_ref[...] = (_rope_blocks(proj(_P_RK, 256), c64, s64) * ret_kscale).astype(BF16)
    rv_ref[...] = proj(_P_RV, 256).astype(BF16)
    rg_ref[...] = proj(_P_RG, 256)


def _const_spec(shape):
    return pl.BlockSpec(shape, lambda i: (0,) * len(shape), pipeline_mode=pl.Buffered(1))


def _pre_attention(x, gain, w_all, wdvt, tabs, qn, wuq, kvn, wuk, wuvt):
    m = x.shape[0]
    bm = PRE_BM
    nb = m // bm
    row = lambda n: pl.BlockSpec((bm, n), lambda i: (i, 0))
    tr = lambda n: pl.BlockSpec((None, n, bm), lambda i: (i, 0, 0))
    out = lambda n, dt=BF16: jax.ShapeDtypeStruct((m, n), dt)
    out_t = lambda n: jax.ShapeDtypeStruct((nb, n, bm), BF16)
    kern = functools.partial(
        _pre_kernel,
        diff_qscale=DIFF_HEAD_DIM ** -0.5 * LOG2E,
        mla_qscale=(MLA_NOPE_DIM + MLA_ROPE_DIM) ** -0.5 * LOG2E,
        ret_kscale=RET_QK_DIM ** -0.5)
    return pl.pallas_call(
        kern,
        grid=(nb,),
        in_specs=[row(D_MODEL), _const_spec((1, D_MODEL)), _const_spec((D_MODEL, _P_COLS)),
                  _const_spec((512, D_MODEL)),
                  row(LANES), row(LANES), row(LANES), row(LANES),
                  _const_spec((1, MLA_Q_RANK)), _const_spec((MLA_Q_RANK, 512)),
                  _const_spec((1, MLA_KV_RANK)), _const_spec((MLA_KV_RANK, 512)),
                  _const_spec((256, MLA_KV_RANK))],
        out_specs=[row(512), row(512), tr(512), row(512), row(512), tr(256),
                   row(256), row(256), row(256), row(256)],
        out_shape=[out(512), out(512), out_t(512), out(512), out(512), out_t(256),
                   out(256), out(256), out(256), out(256, F32)],
        compiler_params=pltpu.CompilerParams(
            dimension_semantics=("parallel",), vmem_limit_bytes=VMEM_LIMIT),
        name="pre_attention",
    )(x, gain, w_all, wdvt, *tabs, qn, wuq, kvn, wuk, wuvt)


def _flash_kernel(*refs, diff, blk, sub, lam_init):
    if diff:
        q_ref, k_ref, vt_ref, lam_ref, subln_ref, o_ref, s_sc, m_sc, l_sc, acc_sc = refs
    else:
        q_ref, k_ref, vt_ref, o_ref, s_sc, m_sc, l_sc, acc_sc = refs
    i = pl.program_id(2)
    nsub = blk // sub

    if diff:
        q = q_ref[...]
        lane = lax.broadcasted_iota(jnp.int32, (1, LANES), 1)
        qs = [jnp.where(((lane % HALF) // 32) == t, q, jnp.zeros_like(q)) for t in range(2)]
        k_lanes = [slice(0, LANES)] * 2
        v_rows = [slice(0, LANES)] * 2
    else:
        qs = [q_ref[:, 0:LANES], q_ref[:, LANES:2 * LANES]]
        k_lanes = [slice(0, LANES), slice(LANES, 2 * LANES)]
        v_rows = [slice(0, MLA_V_DIM), slice(MLA_V_DIM, 2 * MLA_V_DIM)]

    m_sc[...] = jnp.full(m_sc.shape, -jnp.inf, F32)
    l_sc[...] = jnp.zeros(l_sc.shape, F32)
    acc_sc[...] = jnp.zeros(acc_sc.shape, F32)

    def scores(j, slot):
        start = pl.multiple_of(j * blk, blk)
        for t in range(2):
            kj = k_ref[pl.ds(start, blk), k_lanes[t]]
            for c in range(nsub):
                cols = slice(c * sub, (c + 1) * sub)
                s_sc[slot, t, :, cols] = lax.dot_general(
                    kj, qs[t][cols, :], _NT, preferred_element_type=F32)

    def accumulate(j, slot, diagonal):
        for t in range(2):
            for c in range(nsub):
                cols = slice(c * sub, (c + 1) * sub)
                nk = (c + 1) * sub if diagonal else blk
                s = s_sc[slot, t, 0:nk, cols]
                if diagonal:
                    key = lax.broadcasted_iota(jnp.int32, (nk, sub), 0)
                    qry = lax.broadcasted_iota(jnp.int32, (nk, sub), 1) + c * sub
                    s = jnp.where(key <= qry, s, -jnp.inf)
                m_prev = m_sc[t, :, cols]
                m_new = jnp.maximum(m_prev, jnp.max(s, axis=0, keepdims=True))
                alpha = jnp.exp2(m_prev - m_new)
                p = jnp.exp2(s - m_new)
                l_sc[t, :, cols] = alpha * l_sc[t, :, cols] + jnp.sum(p, axis=0, keepdims=True)
                vt = vt_ref[j, v_rows[t], 0:nk]
                acc_sc[t, :, cols] = alpha * acc_sc[t, :, cols] + jnp.dot(
                    vt, p.astype(BF16), preferred_element_type=F32)
                m_sc[t, :, cols] = m_new

    scores(0, 0)

    def pair(jj, carry):
        j0 = 2 * jj
        scores(j0 + 1, 1)
        accumulate(j0, 0, False)
        scores(j0 + 2, 0)
        accumulate(j0 + 1, 1, False)
        return carry

    lax.fori_loop(0, i // 2, pair, 0)

    @pl.when(i % 2 == 0)
    def _():
        accumulate(i, 0, True)

    @pl.when(i % 2 == 1)
    def _():
        scores(i, 1)
        accumulate(i - 1, 0, False)
        accumulate(i, 1, True)

    if diff:
        lv = lam_ref[...]
        lam = (jnp.exp(jnp.sum(lv[0:1] * lv[1:2], axis=1, keepdims=True))
               - jnp.exp(jnp.sum(lv[2:3] * lv[3:4], axis=1, keepdims=True)) + lam_init)
        o = acc_sc[0] * (1.0 / l_sc[0]) - lam * (acc_sc[1] * (1.0 / l_sc[1]))
        ms = jnp.mean(o * o, axis=0, keepdims=True)
        o = o * lax.rsqrt(ms + DIFF_SUBLN_EPS) * subln_ref[...] * (1.0 - lam_init)
    else:
        o = jnp.concatenate([acc_sc[0] * (1.0 / l_sc[0]), acc_sc[1] * (1.0 / l_sc[1])], axis=0)
    o_ref[...] = o.T.astype(o_ref.dtype)


def _flash_attention(q, k, vt, diff, lam=None, subln=None, lam_init=0.0):
    b, s, _ = q.shape
    blk = ATT_BLOCK
    nblk = s // blk
    width = LANES if diff else 2 * LANES
    groups = q.shape[2] // width
    dv = LANES if diff else MLA_V_DIM
    qspec = pl.BlockSpec((None, blk, width), lambda bi, g, i: (bi, i, g))
    kspec = pl.BlockSpec((None, s, width), lambda bi, g, i: (bi, 0, g))
    vspec = pl.BlockSpec((None, nblk, LANES, blk), lambda bi, g, i: (bi, 0, g, 0))
    ospec = pl.BlockSpec((None, blk, LANES), lambda bi, g, i: (bi, i, g))
    in_specs = [qspec, kspec, vspec]
    args = [q, k, vt]
    if diff:
        in_specs += [pl.BlockSpec((4, DIFF_HEAD_DIM), lambda bi, g, i: (0, 0)),
                     pl.BlockSpec((LANES, blk), lambda bi, g, i: (0, 0))]
        args += [lam, subln]
    return pl.pallas_call(
        functools.partial(_flash_kernel, diff=diff, blk=blk, sub=ATT_SUB, lam_init=lam_init),
        grid=(b, groups, nblk),
        in_specs=in_specs,
        out_specs=ospec,
        out_shape=jax.ShapeDtypeStruct((b, s, groups * LANES), BF16),
        scratch_shapes=[pltpu.VMEM((2, 2, blk, blk), F32),
                        pltpu.VMEM((2, 1, blk), F32),
                        pltpu.VMEM((2, 1, blk), F32),
                        pltpu.VMEM((2, dv, blk), F32)],
        compiler_params=pltpu.CompilerParams(
            dimension_semantics=("parallel", "parallel", "arbitrary"),
            vmem_limit_bytes=VMEM_LIMIT),
        name="diff_attention" if diff else "mla_attention",
    )(*args)


def _retention_consts():
    h, c = RET_HEADS, RET_CHUNK
    log_gamma = jnp.log1p(-jnp.exp2(-5.0 - jnp.arange(h, dtype=F32)))
    pos = jnp.arange(c, dtype=F32)
    rel = pos[:, None] - pos[None, :]
    decay = jnp.where(rel[None] >= 0,
                      jnp.exp(jnp.maximum(rel, 0.0)[None] * log_gamma[:, None, None]), 0.0)
    xi = jnp.exp((pos[None, :] + 1.0) * log_gamma[:, None])
    zeta = jnp.exp((c - 1.0 - pos)[None, :] * log_gamma[:, None])
    cdec = jnp.exp(c * log_gamma)
    col = jnp.stack([xi, zeta, jnp.broadcast_to(cdec[:, None], (h, c))], axis=-1)
    return decay.astype(F32), col.astype(F32)


def _retention_kernel(q_ref, k_ref, v_ref, g_ref, dec_ref, col_ref, gw_ref, gb_ref,
                      o_ref, st_sc, *, rows):
    @pl.when(pl.program_id(1) == 0)
    def _():
        st_sc[...] = jnp.zeros(st_sc.shape, F32)

    lane = lax.broadcasted_iota(jnp.int32, (1, LANES), 1)
    low = lane < HALF
    pair_sel = [((lane % HALF) // 32) == hh for hh in range(2)]
    c = RET_CHUNK
    for ci in range(rows // c):
        r0 = ci * c
        for pv in range(RET_HEADS // 2):
            lanes = slice(pv * LANES, (pv + 1) * LANES)
            qp = q_ref[r0:r0 + c, lanes]
            kp = k_ref[r0:r0 + c, lanes]
            vp = v_ref[r0:r0 + c, lanes]
            outs = []
            for hh in range(2):
                h = 2 * pv + hh
                qh = jnp.where(pair_sel[hh], qp, jnp.zeros_like(qp))
                kh = jnp.where(pair_sel[hh], kp, jnp.zeros_like(kp))
                cols = col_ref[h]
                xi, zeta, cdec = cols[:, 0:1], cols[:, 1:2], cols[0:1, 2:3]
                s = lax.dot_general(qh, kh, _NT, preferred_element_type=F32)
                inner = jnp.dot((s * dec_ref[h]).astype(BF16), vp, preferred_element_type=F32)
                state = st_sc[h]
                cross = jnp.dot(qh, state.astype(BF16), preferred_element_type=F32) * xi
                kz = (kh.astype(F32) * zeta).astype(BF16)
                st_sc[h] = (lax.dot_general(kz, vp, (((0,), (0,)), ((), ())),
                                            preferred_element_type=F32) + cdec * state)
                outs.append(inner + cross)
            o = jnp.where(low, outs[0], outs[1])
            inv_n = 1.0 / RET_V_DIM
            s_lo = jnp.sum(jnp.where(low, o, 0.0), axis=1, keepdims=True)
            s_hi = jnp.sum(jnp.where(low, 0.0, o), axis=1, keepdims=True)
            mu = jnp.where(low, s_lo, s_hi) * inv_n
            d = o - mu
            d2 = d * d
            v_lo = jnp.sum(jnp.where(low, d2, 0.0), axis=1, keepdims=True)
            v_hi = jnp.sum(jnp.where(low, 0.0, d2), axis=1, keepdims=True)
            var = jnp.where(low, v_lo, v_hi) * inv_n
            on = d * lax.rsqrt(var + RET_GN_EPS) * gw_ref[:, lanes] + gb_ref[:, lanes]
            g = g_ref[r0:r0 + c, lanes]
            o_ref[r0:r0 + c, lanes] = (g * jax.nn.sigmoid(g) * on).astype(o_ref.dtype)


def _retention(q, k, v, g, decay, col, gw, gb):
    b, s, w = q.shape
    rows = RET_ROWS
    blk = pl.BlockSpec((None, rows, w), lambda bi, i: (bi, i, 0))
    c = RET_CHUNK
    return pl.pallas_call(
        functools.partial(_retention_kernel, rows=rows),
        grid=(b, s // rows),
        in_specs=[blk, blk, blk, blk,
                  pl.BlockSpec((RET_HEADS, c, c), lambda bi, i: (0, 0, 0)),
                  pl.BlockSpec((RET_HEADS, c, 3), lambda bi, i: (0, 0, 0)),
                  pl.BlockSpec((1, w), lambda bi, i: (0, 0)),
                  pl.BlockSpec((1, w), lambda bi, i: (0, 0))],
        out_specs=blk,
        out_shape=jax.ShapeDtypeStruct((b, s, w), BF16),
        scratch_shapes=[pltpu.VMEM((RET_HEADS, LANES, LANES), F32)],
        compiler_params=pltpu.CompilerParams(
            dimension_semantics=("parallel", "arbitrary")),
        name="retention",
    )(q, k, v, g, decay, col, gw, gb)


def _post_kernel(x_ref, od_ref, om_ref, or_ref, wod_ref, wom_ref, wor_ref, g_ref,
                 wg_ref, wu_ref, wd_ref, fg_ref, o_ref, a_sc, *, final):
    x1 = (x_ref[...]
          + jnp.dot(od_ref[...], wod_ref[...], preferred_element_type=F32)
          + jnp.dot(om_ref[...], wom_ref[...], preferred_element_type=F32)
          + jnp.dot(or_ref[...], wor_ref[...], preferred_element_type=F32))
    hb = _rms(x1, g_ref[...], NORM_EPS).astype(BF16)
    for c0 in range(0, FFN_HIDDEN, FFN_CHUNK):
        gate = jnp.dot(hb, wg_ref[:, c0:c0 + FFN_CHUNK], preferred_element_type=F32)
        up = jnp.dot(hb, wu_ref[:, c0:c0 + FFN_CHUNK], preferred_element_type=F32)
        a_sc[:, c0:c0 + FFN_CHUNK] = (gate * jax.nn.sigmoid(gate) * up).astype(BF16)
    y = x1 + jnp.dot(a_sc[...], wd_ref[...], preferred_element_type=F32)
    if final:
        y = _rms(y, fg_ref[...], NORM_EPS)
    o_ref[...] = y


def _post_attention(x, od, om, orr, wod, wom, wor, gain, wg, wu, wd, fgain, final):
    m = x.shape[0]
    bm = POST_BM
    row = lambda n: pl.BlockSpec((bm, n), lambda i: (i, 0))
    return pl.pallas_call(
        functools.partial(_post_kernel, final=final),
        grid=(m // bm,),
        in_specs=[row(D_MODEL), row(512), row(256), row(256),
                  _const_spec((512, D_MODEL)), _const_spec((256, D_MODEL)),
                  _const_spec((256, D_MODEL)), _const_spec((1, D_MODEL)),
                  _const_spec((D_MODEL, FFN_HIDDEN)), _const_spec((D_MODEL, FFN_HIDDEN)),
                  _const_spec((FFN_HIDDEN, D_MODEL)), _const_spec((1, D_MODEL))],
        out_specs=row(D_MODEL),
        out_shape=jax.ShapeDtypeStruct((m, D_MODEL), F32),
        scratch_shapes=[pltpu.VMEM((bm, FFN_HIDDEN), BF16)],
        compiler_params=pltpu.CompilerParams(
            dimension_semantics=("parallel",), vmem_limit_bytes=VMEM_LIMIT),
        name="post_attention",
    )(x, od, om, orr, wod, wom, wor, gain, wg, wu, wd, fgain)


def kernel(x, positions, attn_norm, w_in, diff_lam_q1, diff_lam_k1, diff_lam_q2, diff_lam_k2, diff_subln, mla_q_norm, mla_w_uq, mla_kv_norm, mla_w_ukv, ret_gn_w, ret_gn_b, w_out, ffn_norm, w_gate, w_up, w_down, final_norm):
    b, s, d = x.shape
    m = b * s
    nblk = s // ATT_BLOCK
    invf, sgn = _rope_lane_params()
    tabs = _rope_tables(positions.astype(F32).reshape(m, 1), invf, sgn)
    decay, col = _retention_consts()

    xf = x.reshape(m, d)
    for layer in range(DEPTH):
        lam_init = 0.8 - 0.6 * math.exp(-0.3 * layer)
        w_all = _in_proj_layout(w_in[layer]).astype(BF16)
        wdvt = w_in[layer][:, _OFF_DV:_OFF_DV + 512].T.astype(BF16)
        wuq = _mla_q_layout(mla_w_uq[layer]).astype(BF16)
        wuk, wuv = _mla_kv_layout(mla_w_ukv[layer])
        (dq, dk, dvt, mq, mk, mvt, rq, rk, rv, rg) = _pre_attention(
            xf, attn_norm[layer][None], w_all, wdvt, tabs,
            mla_q_norm[layer][None], wuq, mla_kv_norm[layer][None],
            wuk.astype(BF16), wuv.T.astype(BF16))

        sh = lambda t: t.reshape(b, s, t.shape[-1])
        sh_t = lambda t: t.reshape(b, nblk, t.shape[1], ATT_BLOCK)
        lam = jnp.stack([diff_lam_q1[layer], diff_lam_k1[layer],
                         diff_lam_q2[layer], diff_lam_k2[layer]])
        subln = jnp.broadcast_to(diff_subln[layer][:, None], (LANES, ATT_BLOCK))
        o_diff = _flash_attention(sh(dq), sh(dk), sh_t(dvt), True, lam=lam,
                                  subln=subln, lam_init=lam_init)
        o_mla = _flash_attention(sh(mq), sh(mk), sh_t(mvt), False)
        o_ret = _retention(sh(rq), sh(rk), sh(rv), sh(rg), decay, col,
                           ret_gn_w[layer][None], ret_gn_b[layer][None])

        wo = w_out[layer].astype(BF16)
        xf = _post_attention(
            xf, o_diff.reshape(m, 512), o_mla.reshape(m, 256), o_ret.reshape(m, 256),
            wo[:512], wo[512:768], wo[768:], ffn_norm[layer][None],
            w_gate[layer].astype(BF16), w_up[layer].astype(BF16), w_down[layer].astype(BF16),
            final_norm[None], final=(layer == DEPTH - 1))
    return xf.reshape(b, s, d)
```

```python
import functools
import math

import jax
import jax.numpy as jnp
from jax import lax
from jax.experimental import pallas as pl
from jax.experimental.pallas import tpu as pltpu

F32 = jnp.float32
BF16 = jnp.bfloat16

D_MODEL = 1024
DEPTH = 2
ROPE_THETA = 10000.0
NORM_EPS = 1e-6

DIFF_HEADS = 4
DIFF_HEAD_DIM = 64
DIFF_SUBLN_EPS = 1e-5
MLA_HEADS = 4
MLA_NOPE_DIM = 64
MLA_ROPE_DIM = 32
MLA_V_DIM = 64
MLA_Q_RANK = 256
MLA_KV_RANK = 128
RET_HEADS = 4
RET_QK_DIM = 64
RET_V_DIM = 64
RET_CHUNK = 128
RET_GN_EPS = 1e-6
FFN_HIDDEN = 2816

LANES = 128
HALF = LANES // 2
LOG2E = math.log2(math.e)

_OFF_DQ, _OFF_DK, _OFF_DV = 0, 512, 1024
_OFF_CQ, _OFF_CKV, _OFF_KR = 1536, 1792, 1920
_OFF_RQ, _OFF_RK, _OFF_RV, _OFF_RG = 1952, 2208, 2464, 2720

ATT_BLOCK = 512
ATT_SUB = 256
PRE_BM = ATT_BLOCK
POST_BM = 512
RET_ROWS = 512
FFN_CHUNK = 256
VMEM_LIMIT = 56 * 1024 * 1024

_NT = (((1,), (1,)), ((), ()))


def _pair_layout(w):
    k, n = w.shape
    return w.reshape(k, n // LANES, 2, 2, 32).transpose(0, 1, 3, 2, 4).reshape(k, n)


def _mla_q_layout(w):
    k = w.shape[0]
    w = w.reshape(k, MLA_HEADS, MLA_NOPE_DIM + MLA_ROPE_DIM)
    nope = w[..., :MLA_NOPE_DIM].reshape(k, MLA_HEADS, 2, 32)
    rope = w[..., MLA_NOPE_DIM:].reshape(k, MLA_HEADS, 2, 16)
    pad = jnp.zeros((k, MLA_HEADS, 2, 16), w.dtype)
    return jnp.concatenate([nope, rope, pad], axis=-1).reshape(k, MLA_HEADS * LANES)


def _mla_kv_layout(w):
    k = w.shape[0]
    w = w.reshape(k, MLA_HEADS, MLA_NOPE_DIM + MLA_V_DIM)
    nope = w[..., :MLA_NOPE_DIM].reshape(k, MLA_HEADS, 2, 32)
    pad = jnp.zeros((k, MLA_HEADS, 2, 32), w.dtype)
    wk = jnp.concatenate([nope, pad], axis=-1).reshape(k, MLA_HEADS * LANES)
    wv = w[..., MLA_NOPE_DIM:].reshape(k, MLA_HEADS * MLA_V_DIM)
    return wk, wv


def _mla_krope_layout(w):
    k = w.shape[0]
    kr = w.reshape(k, 2, 16)
    return jnp.concatenate([jnp.zeros((k, 2, 32), w.dtype), kr,
                            jnp.zeros((k, 2, 16), w.dtype)], axis=-1).reshape(k, LANES)


def _in_proj_layout(w):
    sl = lambda off, n: w[:, off:off + n]
    return jnp.concatenate([
        _pair_layout(sl(_OFF_DQ, 512)), _pair_layout(sl(_OFF_DK, 512)),
        sl(_OFF_CQ, MLA_Q_RANK), sl(_OFF_CKV, MLA_KV_RANK),
        _mla_krope_layout(sl(_OFF_KR, MLA_ROPE_DIM)),
        _pair_layout(sl(_OFF_RQ, 256)), _pair_layout(sl(_OFF_RK, 256)),
        sl(_OFF_RV, 256), sl(_OFF_RG, 256)], axis=1)


_P_DQ, _P_DK = 0, 512
_P_CQ, _P_CKV, _P_KR = 1024, 1280, 1408
_P_RQ, _P_RK, _P_RV, _P_RG = 1536, 1792, 2048, 2304
_P_COLS = 2560


def _rope_lane_params():
    lane = jnp.arange(LANES)
    f64 = 1.0 / (ROPE_THETA ** (jnp.arange(0, 64, 2, dtype=F32) / 64))
    f32 = 1.0 / (ROPE_THETA ** (jnp.arange(0, 32, 2, dtype=F32) / 32))
    sign = jnp.where(lane < HALF, -1.0, 1.0).astype(F32)
    inv_a = jnp.take(f64, lane % 32)
    is_rope = ((lane % HALF) >= 32) & ((lane % HALF) < 48)
    inv_b = jnp.where(is_rope, jnp.take(f32, lane % 16), 0.0)
    return jnp.stack([inv_a, inv_b]).astype(F32), jnp.stack([sign, sign])


def _rope_table_kernel(pos_ref, invf_ref, sgn_ref, c64_ref, s64_ref, cm_ref, sm_ref):
    pos = pos_ref[...]
    a = pos * invf_ref[0:1, :]
    c64_ref[...] = jnp.cos(a)
    s64_ref[...] = jnp.sin(a) * sgn_ref[0:1, :]
    b = pos * invf_ref[1:2, :]
    cm_ref[...] = jnp.cos(b)
    sm_ref[...] = jnp.sin(b) * sgn_ref[1:2, :]


def _rope_tables(pos, invf, sgn):
    m = pos.shape[0]
    bm = 2048
    tab = jax.ShapeDtypeStruct((m, LANES), F32)
    row = pl.BlockSpec((bm, LANES), lambda i: (i, 0))
    return pl.pallas_call(
        _rope_table_kernel,
        grid=(m // bm,),
        in_specs=[pl.BlockSpec((bm, 1), lambda i: (i, 0)),
                  pl.BlockSpec((2, LANES), lambda i: (0, 0)),
                  pl.BlockSpec((2, LANES), lambda i: (0, 0))],
        out_specs=[row, row, row, row],
        out_shape=[tab, tab, tab, tab],
        name="rope_tables",
    )(pos, invf, sgn)


def _rms(t, gain, eps):
    return t * lax.rsqrt(jnp.mean(t * t, axis=-1, keepdims=True) + eps) * gain


def _rope_blocks(y, cos, sin):
    outs = []
    for j in range(y.shape[1] // LANES):
        blk = y[:, j * LANES:(j + 1) * LANES]
        outs.append(blk * cos + pltpu.roll(blk, HALF, 1) * sin)
    return outs[0] if len(outs) == 1 else jnp.concatenate(outs, axis=1)


def _pre_kernel(x_ref, g_ref, w_ref, wdvt_ref, c64_ref, s64_ref, cm_ref, sm_ref,
                qn_ref, wuq_ref, kvn_ref, wuk_ref, wuvt_ref,
                dq_ref, dk_ref, dvt_ref, mq_ref, mk_ref, mvt_ref,
                rq_ref, rk_ref, rv_ref, rg_ref, *, diff_qscale, mla_qscale, ret_kscale):
    hb = _rms(x_ref[...], g_ref[...], NORM_EPS).astype(BF16)

    def proj(off, n):
        return jnp.dot(hb, w_ref[:, off:off + n], preferred_element_type=F32)

    c64, s64 = c64_ref[...], s64_ref[...]
    cm, sm = cm_ref[...], sm_ref[...]

    dq_ref[...] = (_rope_blocks(proj(_P_DQ, 512), c64, s64) * diff_qscale).astype(BF16)
    dk_ref[...] = _rope_blocks(proj(_P_DK, 512), c64, s64).astype(BF16)
    dvt_ref[...] = lax.dot_general(wdvt_ref[...], hb, _NT,
                                   preferred_element_type=F32).astype(BF16)

    cq = _rms(proj(_P_CQ, MLA_Q_RANK), qn_ref[...], NORM_EPS).astype(BF16)
    q = jnp.dot(cq, wuq_ref[...], preferred_element_type=F32)
    mq_ref[...] = (_rope_blocks(q, cm, sm) * mla_qscale).astype(BF16)
    ckv = _rms(proj(_P_CKV, MLA_KV_RANK), kvn_ref[...], NORM_EPS).astype(BF16)
    kn = jnp.dot(ckv, wuk_ref[...], preferred_element_type=F32)
    kr = _rope_blocks(proj(_P_KR, LANES), cm, sm)
    mk_ref[...] = (kn + jnp.concatenate([kr] * MLA_HEADS, axis=1)).astype(BF16)
    mvt_ref[...] = lax.dot_general(wuvt_ref[...], ckv, _NT,
                                   preferred_element_type=F32).astype(BF16)

    rq_ref[...] = _rope_blocks(proj(_P_RQ, 256), c64, s64).astype(BF16)
    rk_ref[...] = (_rope_blocks(proj(_P_RK, 256), c64, s64) * ret_kscale).astype(BF16)
    rv_ref[...] = proj(_P_RV, 256).astype(BF16)
    rg_ref[...] = proj(_P_RG, 256)


def _const_spec(shape):
    return pl.BlockSpec(shape, lambda i: (0,) * len(shape), pipeline_mode=pl.Buffered(1))


def _pre_attention(x, gain, w_all, wdvt, tabs, qn, wuq, kvn, wuk, wuvt):
    m = x.shape[0]
    bm = PRE_BM
    nb = m // bm
    row = lambda n: pl.BlockSpec((bm, n), lambda i: (i, 0))
    tr = lambda n: pl.BlockSpec((None, n, bm), lambda i: (i, 0, 0))
    out = lambda n, dt=BF16: jax.ShapeDtypeStruct((m, n), dt)
    out_t = lambda n: jax.ShapeDtypeStruct((nb, n, bm), BF16)
    kern = functools.partial(
        _pre_kernel,
        diff_qscale=DIFF_HEAD_DIM ** -0.5 * LOG2E,
        mla_qscale=(MLA_NOPE_DIM + MLA_ROPE_DIM) ** -0.5 * LOG2E,
        ret_kscale=RET_QK_DIM ** -0.5)
    return pl.pallas_call(
        kern,
        grid=(nb,),
        in_specs=[row(D_MODEL), _const_spec((1, D_MODEL)), _const_spec((D_MODEL, _P_COLS)),
                  _const_spec((512, D_MODEL)),
                  row(LANES), row(LANES), row(LANES), row(LANES),
                  _const_spec((1, MLA_Q_RANK)), _const_spec((MLA_Q_RANK, 512)),
                  _const_spec((1, MLA_KV_RANK)), _const_spec((MLA_KV_RANK, 512)),
                  _const_spec((256, MLA_KV_RANK))],
        out_specs=[row(512), row(512), tr(512), row(512), row(512), tr(256),
                   row(256), row(256), row(256), row(256)],
        out_shape=[out(512), out(512), out_t(512), out(512), out(512), out_t(256),
                   out(256), out(256), out(256), out(256, F32)],
        compiler_params=pltpu.CompilerParams(
            dimension_semantics=("parallel",), vmem_limit_bytes=VMEM_LIMIT),
        name="pre_attention",
    )(x, gain, w_all, wdvt, *tabs, qn, wuq, kvn, wuk, wuvt)


def _flash_kernel(*refs, diff, blk, sub, nblk, lam_init):
    if diff:
        q_ref, k_ref, vt_ref, lam_ref, subln_ref, o_ref, s_sc, mx_sc, m_sc, l_sc, acc_sc = refs
    else:
        q_ref, k_ref, vt_ref, o_ref, s_sc, mx_sc, m_sc, l_sc, acc_sc = refs
    nsub = blk // sub
    entry = 2

    if diff:
        lane = lax.broadcasted_iota(jnp.int32, (1, LANES), 1)
        k_lanes = [slice(0, LANES)] * 2
        v_rows = [slice(0, LANES)] * 2
    else:
        k_lanes = [slice(0, LANES), slice(LANES, 2 * LANES)]
        v_rows = [slice(0, MLA_V_DIM), slice(MLA_V_DIM, 2 * MLA_V_DIM)]

    def load_q(i):
        start = pl.multiple_of(i * blk, blk)
        if diff:
            q = q_ref[pl.ds(start, blk), :]
            return [jnp.where(((lane % HALF) // 32) == t, q, jnp.zeros_like(q))
                    for t in range(2)]
        return [q_ref[pl.ds(start, blk), k_lanes[t]] for t in range(2)]

    def scores(qs, j, slot):
        start = pl.multiple_of(j * blk, blk)
        for t in range(2):
            kj = k_ref[pl.ds(start, blk), k_lanes[t]]
            for c in range(nsub):
                cols = slice(c * sub, (c + 1) * sub)
                s = lax.dot_general(kj, qs[t][cols, :], _NT, preferred_element_type=F32)
                s_sc[slot, t, c] = s
                mx_sc[slot, t, :, cols] = jnp.max(s, axis=0, keepdims=True)

    def accumulate(j, slot, diagonal):
        for t in range(2):
            for c in range(nsub):
                cols = slice(c * sub, (c + 1) * sub)
                nk = (c + 1) * sub if diagonal else blk
                s = s_sc[slot, t, c, 0:nk, :]
                if diagonal:
                    key = lax.broadcasted_iota(jnp.int32, (nk, sub), 0)
                    qry = lax.broadcasted_iota(jnp.int32, (nk, sub), 1) + c * sub
                    s = jnp.where(key <= qry, s, -jnp.inf)
                    m_cur = jnp.max(s, axis=0, keepdims=True)
                else:
                    m_cur = mx_sc[slot, t, :, cols]
                m_prev = m_sc[t, :, cols]
                m_new = jnp.maximum(m_prev, m_cur)
                alpha = jnp.exp2(m_prev - m_new)
                p = jnp.exp2(s - m_new)
                l_sc[t, :, cols] = alpha * l_sc[t, :, cols] + jnp.sum(p, axis=0, keepdims=True)
                vt = vt_ref[j, v_rows[t], 0:nk]
                acc_sc[t, c] = alpha * acc_sc[t, c] + jnp.dot(
                    vt, p.astype(BF16), preferred_element_type=F32)
                m_sc[t, :, cols] = m_new

    if diff:
        lv = lam_ref[...]
        lam = (jnp.exp(jnp.sum(lv[0:1] * lv[1:2], axis=1, keepdims=True))
               - jnp.exp(jnp.sum(lv[2:3] * lv[3:4], axis=1, keepdims=True)) + lam_init)

    def finalize(i):
        for c in range(nsub):
            cols = slice(c * sub, (c + 1) * sub)
            o0 = acc_sc[0, c] * (1.0 / l_sc[0, :, cols])
            o1 = acc_sc[1, c] * (1.0 / l_sc[1, :, cols])
            if diff:
                o = o0 - lam * o1
                ms = jnp.mean(o * o, axis=0, keepdims=True)
                o = o * lax.rsqrt(ms + DIFF_SUBLN_EPS) * subln_ref[:, cols] * (1.0 - lam_init)
            else:
                o = jnp.concatenate([o0, o1], axis=0)
            row = pl.multiple_of(i * blk + c * sub, sub)
            o_ref[pl.ds(row, sub), :] = o.T.astype(o_ref.dtype)

    scores(load_q(0), 0, entry)

    def query_block(i, carry):
        qs = load_q(i)
        i_next = jnp.minimum(i + 1, nblk - 1)
        qs_next = load_q(i_next)
        m_sc[...] = jnp.full(m_sc.shape, -jnp.inf, F32)
        l_sc[...] = jnp.zeros(l_sc.shape, F32)
        acc_sc[...] = jnp.zeros(acc_sc.shape, F32)

        @pl.when(i == 0)
        def _():
            accumulate(i, entry, True)
            scores(qs_next, i_next, entry)

        @pl.when(i > 0)
        def _():
            scores(qs, 0, 0)
            accumulate(i, entry, True)

        def pair(jj, c2):
            j0 = 2 * jj
            scores(qs, j0 + 1, 1)
            accumulate(j0, 0, False)
            scores(qs, j0 + 2, 0)
            accumulate(j0 + 1, 1, False)
            return c2

        lax.fori_loop(0, (i - 1) // 2, pair, 0)

        @pl.when(i % 2 == 1)
        def _():
            scores(qs_next, i_next, entry)
            accumulate(i - 1, 0, False)

        @pl.when(jnp.logical_and(i % 2 == 0, i > 0))
        def _():
            scores(qs, i - 1, 1)
            accumulate(i - 2, 0, False)
            scores(qs_next, i_next, entry)
            accumulate(i - 1, 1, False)

        finalize(i)
        return carry

    lax.fori_loop(0, nblk, query_block, 0)


def _flash_attention(q, k, vt, diff, lam=None, subln=None, lam_init=0.0):
    b, s, _ = q.shape
    blk, sub = ATT_BLOCK, ATT_SUB
    nblk, nsub = s // blk, blk // sub
    width = LANES if diff else 2 * LANES
    groups = q.shape[2] // width
    dv = LANES if diff else MLA_V_DIM
    qkspec = pl.BlockSpec((None, s, width), lambda bi, g: (bi, 0, g))
    vspec = pl.BlockSpec((None, nblk, LANES, blk), lambda bi, g: (bi, 0, g, 0))
    ospec = pl.BlockSpec((None, s, LANES), lambda bi, g: (bi, 0, g))
    in_specs = [qkspec, qkspec, vspec]
    args = [q, k, vt]
    if diff:
        in_specs += [pl.BlockSpec((4, DIFF_HEAD_DIM), lambda bi, g: (0, 0)),
                     pl.BlockSpec((LANES, blk), lambda bi, g: (0, 0))]
        args += [lam, subln]
    return pl.pallas_call(
        functools.partial(_flash_kernel, diff=diff, blk=blk, sub=sub, nblk=nblk,
                          lam_init=lam_init),
        grid=(b, groups),
        in_specs=in_specs,
        out_specs=ospec,
        out_shape=jax.ShapeDtypeStruct((b, s, groups * LANES), BF16),
        scratch_shapes=[pltpu.VMEM((3, 2, nsub, blk, sub), F32),
                        pltpu.VMEM((3, 2, 1, blk), F32),
                        pltpu.VMEM((2, 1, blk), F32),
                        pltpu.VMEM((2, 1, blk), F32),
                        pltpu.VMEM((2, nsub, dv, sub), F32)],
        compiler_params=pltpu.CompilerParams(
            dimension_semantics=("parallel", "parallel"),
            vmem_limit_bytes=VMEM_LIMIT),
        name="diff_attention" if diff else "mla_attention",
    )(*args)


def _retention_consts():
    h, c = RET_HEADS, RET_CHUNK
    log_gamma = jnp.log1p(-jnp.exp2(-5.0 - jnp.arange(h, dtype=F32)))
    pos = jnp.arange(c, dtype=F32)
    rel = pos[:, None] - pos[None, :]
    decay = jnp.where(rel[None] >= 0,
                      jnp.exp(jnp.maximum(rel, 0.0)[None] * log_gamma[:, None, None]), 0.0)
    xi = jnp.exp((pos[None, :] + 1.0) * log_gamma[:, None])
    zeta = jnp.exp((c - 1.0 - pos)[None, :] * log_gamma[:, None])
    cdec = jnp.exp(c * log_gamma)
    col = jnp.stack([xi, zeta, jnp.broadcast_to(cdec[:, None], (h, c))], axis=-1)
    return decay.astype(F32), col.astype(F32)


def _retention_kernel(q_ref, k_ref, v_ref, g_ref, dec_ref, col_ref, gw_ref, gb_ref,
                      o_ref, st_sc, *, rows):
    @pl.when(pl.program_id(1) == 0)
    def _():
        st_sc[...] = jnp.zeros(st_sc.shape, F32)

    lane = lax.broadcasted_iota(jnp.int32, (1, LANES), 1)
    low = lane < HALF
    pair_sel = [((lane % HALF) // 32) == hh for hh in range(2)]
    c = RET_CHUNK
    for ci in range(rows // c):
        r0 = ci * c
        for pv in range(RET_HEADS // 2):
            lanes = slice(pv * LANES, (pv + 1) * LANES)
            qp = q_ref[r0:r0 + c, lanes]
            kp = k_ref[r0:r0 + c, lanes]
            vp = v_ref[r0:r0 + c, lanes]
            outs = []
            for hh in range(2):
                h = 2 * pv + hh
                qh = jnp.where(pair_sel[hh], qp, jnp.zeros_like(qp))
                kh = jnp.where(pair_sel[hh], kp, jnp.zeros_like(kp))
                cols = col_ref[h]
                xi, zeta, cdec = cols[:, 0:1], cols[:, 1:2], cols[0:1, 2:3]
                s = lax.dot_general(qh, kh, _NT, preferred_element_type=F32)
                inner = jnp.dot((s * dec_ref[h]).astype(BF16), vp, preferred_element_type=F32)
                state = st_sc[h]
                cross = jnp.dot(qh, state.astype(BF16), preferred_element_type=F32) * xi
                kz = (kh.astype(F32) * zeta).astype(BF16)
                st_sc[h] = (lax.dot_general(kz, vp, (((0,), (0,)), ((), ())),
                                            preferred_element_type=F32) + cdec * state)
                outs.append(inner + cross)
            o = jnp.where(low, outs[0], outs[1])
            inv_n = 1.0 / RET_V_DIM
            s_lo = jnp.sum(jnp.where(low, o, 0.0), axis=1, keepdims=True)
            s_hi = jnp.sum(jnp.where(low, 0.0, o), axis=1, keepdims=True)
            mu = jnp.where(low, s_lo, s_hi) * inv_n
            d = o - mu
            d2 = d * d
            v_lo = jnp.sum(jnp.where(low, d2, 0.0), axis=1, keepdims=True)
            v_hi = jnp.sum(jnp.where(low, 0.0, d2), axis=1, keepdims=True)
            var = jnp.where(low, v_lo, v_hi) * inv_n
            on = d * lax.rsqrt(var + RET_GN_EPS) * gw_ref[:, lanes] + gb_ref[:, lanes]
            g = g_ref[r0:r0 + c, lanes]
            o_ref[r0:r0 + c, lanes] = (g * jax.nn.sigmoid(g) * on).astype(o_ref.dtype)


def _retention(q, k, v, g, decay, col, gw, gb):
    b, s, w = q.shape
    rows = RET_ROWS
    blk = pl.BlockSpec((None, rows, w), lambda bi, i: (bi, i, 0))
    c = RET_CHUNK
    return pl.pallas_call(
        functools.partial(_retention_kernel, rows=rows),
        grid=(b, s // rows),
        in_specs=[blk, blk, blk, blk,
                  pl.BlockSpec((RET_HEADS, c, c), lambda bi, i: (0, 0, 0)),
                  pl.BlockSpec((RET_HEADS, c, 3), lambda bi, i: (0, 0, 0)),
                  pl.BlockSpec((1, w), lambda bi, i: (0, 0)),
                  pl.BlockSpec((1, w), lambda bi, i: (0, 0))],
        out_specs=blk,
        out_shape=jax.ShapeDtypeStruct((b, s, w), BF16),
        scratch_shapes=[pltpu.VMEM((RET_HEADS, LANES, LANES), F32)],
        compiler_params=pltpu.CompilerParams(
            dimension_semantics=("parallel", "arbitrary")),
        name="retention",
    )(q, k, v, g, decay, col, gw, gb)


def _post_kernel(x_ref, od_ref, om_ref, or_ref, wod_ref, wom_ref, wor_ref, g_ref,
                 wg_ref, wu_ref, wd_ref, fg_ref, o_ref, a_sc, *, final):
    x1 = (x_ref[...]
          + jnp.dot(od_ref[...], wod_ref[...], preferred_element_type=F32)
          + jnp.dot(om_ref[...], wom_ref[...], preferred_element_type=F32)
          + jnp.dot(or_ref[...], wor_ref[...], preferred_element_type=F32))
    hb = _rms(x1, g_ref[...], NORM_EPS).astype(BF16)
    for c0 in range(0, FFN_HIDDEN, FFN_CHUNK):
        gate = jnp.dot(hb, wg_ref[:, c0:c0 + FFN_CHUNK], preferred_element_type=F32)
        up = jnp.dot(hb, wu_ref[:, c0:c0 + FFN_CHUNK], preferred_element_type=F32)
        a_sc[:, c0:c0 + FFN_CHUNK] = (gate * jax.nn.sigmoid(gate) * up).astype(BF16)
    y = x1 + jnp.dot(a_sc[...], wd_ref[...], preferred_element_type=F32)
    if final:
        y = _rms(y, fg_ref[...], NORM_EPS)
    o_ref[...] = y


def _post_attention(x, od, om, orr, wod, wom, wor, gain, wg, wu, wd, fgain, final):
    m = x.shape[0]
    bm = POST_BM
    row = lambda n: pl.BlockSpec((bm, n), lambda i: (i, 0))
    return pl.pallas_call(
        functools.partial(_post_kernel, final=final),
        grid=(m // bm,),
        in_specs=[row(D_MODEL), row(512), row(256), row(256),
                  _const_spec((512, D_MODEL)), _const_spec((256, D_MODEL)),
                  _const_spec((256, D_MODEL)), _const_spec((1, D_MODEL)),
                  _const_spec((D_MODEL, FFN_HIDDEN)), _const_spec((D_MODEL, FFN_HIDDEN)),
                  _const_spec((FFN_HIDDEN, D_MODEL)), _const_spec((1, D_MODEL))],
        out_specs=row(D_MODEL),
        out_shape=jax.ShapeDtypeStruct((m, D_MODEL), F32),
        scratch_shapes=[pltpu.VMEM((bm, FFN_HIDDEN), BF16)],
        compiler_params=pltpu.CompilerParams(
            dimension_semantics=("parallel",), vmem_limit_bytes=VMEM_LIMIT),
        name="post_attention",
    )(x, od, om, orr, wod, wom, wor, gain, wg, wu, wd, fgain)


def kernel(x, positions, attn_norm, w_in, diff_lam_q1, diff_lam_k1, diff_lam_q2, diff_lam_k2, diff_subln, mla_q_norm, mla_w_uq, mla_kv_norm, mla_w_ukv, ret_gn_w, ret_gn_b, w_out, ffn_norm, w_gate, w_up, w_down, final_norm):
    b, s, d = x.shape
    m = b * s
    nblk = s // ATT_BLOCK
    invf, sgn = _rope_lane_params()
    tabs = _rope_tables(positions.astype(F32).reshape(m, 1), invf, sgn)
    decay, col = _retention_consts()

    xf = x.reshape(m, d)
    for layer in range(DEPTH):
        lam_init = 0.8 - 0.6 * math.exp(-0.3 * layer)
        w_all = _in_proj_layout(w_in[layer]).astype(BF16)
        wdvt = w_in[layer][:, _OFF_DV:_OFF_DV + 512].T.astype(BF16)
        wuq = _mla_q_layout(mla_w_uq[layer]).astype(BF16)
        wuk, wuv = _mla_kv_layout(mla_w_ukv[layer])
        (dq, dk, dvt, mq, mk, mvt, rq, rk, rv, rg) = _pre_attention(
            xf, attn_norm[layer][None], w_all, wdvt, tabs,
            mla_q_norm[layer][None], wuq, mla_kv_norm[layer][None],
            wuk.astype(BF16), wuv.T.astype(BF16))

        sh = lambda t: t.reshape(b, s, t.shape[-1])
        sh_t = lambda t: t.reshape(b, nblk, t.shape[1], ATT_BLOCK)
        lam = jnp.stack([diff_lam_q1[layer], diff_lam_k1[layer],
                         diff_lam_q2[layer], diff_lam_k2[layer]])
        subln = jnp.broadcast_to(diff_subln[layer][:, None], (LANES, ATT_BLOCK))
        o_diff = _flash_attention(sh(dq), sh(dk), sh_t(dvt), True, lam=lam,
                                  subln=subln, lam_init=lam_init)
        o_mla = _flash_attention(sh(mq), sh(mk), sh_t(mvt), False)
        o_ret = _retention(sh(rq), sh(rk), sh(rv), sh(rg), decay, col,
                           ret_gn_w[layer][None], ret_gn_b[layer][None])

        wo = w_out[layer].astype(BF16)
        xf = _post_attention(
            xf, o_diff.reshape(m, 512), o_mla.reshape(m, 256), o_ret.reshape(m, 256),
            wo[:512], wo[512:768], wo[768:], ffn_norm[layer][None],
            w_gate[layer].astype(BF16), w_up[layer].astype(BF16), w_down[layer].astype(BF16),
            final_norm[None], final=(layer == DEPTH - 1))
    return xf.reshape(b, s, d)
```

```python
import functools
import math

import jax
import jax.numpy as jnp
from jax import lax
from jax.experimental import pallas as pl
from jax.experimental.pallas import tpu as pltpu

F32 = jnp.float32
BF16 = jnp.bfloat16

D_MODEL = 1024
DEPTH = 2
ROPE_THETA = 10000.0
NORM_EPS = 1e-6

DIFF_HEADS = 4
DIFF_HEAD_DIM = 64
DIFF_SUBLN_EPS = 1e-5
MLA_HEADS = 4
MLA_NOPE_DIM = 64
MLA_ROPE_DIM = 32
MLA_V_DIM = 64
MLA_Q_RANK = 256
MLA_KV_RANK = 128
RET_HEADS = 4
RET_QK_DIM = 64
RET_V_DIM = 64
RET_CHUNK = 128
RET_GN_EPS = 1e-6
FFN_HIDDEN = 2816

LANES = 128
HALF = LANES // 2
LOG2E = math.log2(math.e)

_OFF_DQ, _OFF_DK, _OFF_DV = 0, 512, 1024
_OFF_CQ, _OFF_CKV, _OFF_KR = 1536, 1792, 1920
_OFF_RQ, _OFF_RK, _OFF_RV, _OFF_RG = 1952, 2208, 2464, 2720

ATT_BLOCK = 512
ATT_SUB = 256
PRE_BM = ATT_BLOCK
POST_BM = 512
RET_ROWS = 512
FFN_CHUNK = 256
VMEM_LIMIT = 56 * 1024 * 1024

_NT = (((1,), (1,)), ((), ()))


def _pair_layout(w):
    k, n = w.shape
    return w.reshape(k, n // LANES, 2, 2, 32).transpose(0, 1, 3, 2, 4).reshape(k, n)


def _mla_q_layout(w):
    k = w.shape[0]
    w = w.reshape(k, MLA_HEADS, MLA_NOPE_DIM + MLA_ROPE_DIM)
    nope = w[..., :MLA_NOPE_DIM].reshape(k, MLA_HEADS, 2, 32)
    rope = w[..., MLA_NOPE_DIM:].reshape(k, MLA_HEADS, 2, 16)
    pad = jnp.zeros((k, MLA_HEADS, 2, 16), w.dtype)
    return jnp.concatenate([nope, rope, pad], axis=-1).reshape(k, MLA_HEADS * LANES)


def _mla_kv_layout(w):
    k = w.shape[0]
    w = w.reshape(k, MLA_HEADS, MLA_NOPE_DIM + MLA_V_DIM)
    nope = w[..., :MLA_NOPE_DIM].reshape(k, MLA_HEADS, 2, 32)
    pad = jnp.zeros((k, MLA_HEADS, 2, 32), w.dtype)
    wk = jnp.concatenate([nope, pad], axis=-1).reshape(k, MLA_HEADS * LANES)
    wv = w[..., MLA_NOPE_DIM:].reshape(k, MLA_HEADS * MLA_V_DIM)
    return wk, wv


def _mla_krope_layout(w):
    k = w.shape[0]
    kr = w.reshape(k, 2, 16)
    return jnp.concatenate([jnp.zeros((k, 2, 32), w.dtype), kr,
                            jnp.zeros((k, 2, 16), w.dtype)], axis=-1).reshape(k, LANES)


def _in_proj_layout(w):
    sl = lambda off, n: w[:, off:off + n]
    return jnp.concatenate([
        _pair_layout(sl(_OFF_DQ, 512)), _pair_layout(sl(_OFF_DK, 512)),
        sl(_OFF_CQ, MLA_Q_RANK), sl(_OFF_CKV, MLA_KV_RANK),
        _mla_krope_layout(sl(_OFF_KR, MLA_ROPE_DIM)),
        _pair_layout(sl(_OFF_RQ, 256)), _pair_layout(sl(_OFF_RK, 256))], axis=1)


def _in_proj_layout_t(w):
    return jnp.concatenate([w[:, _OFF_DV:_OFF_DV + 512], w[:, _OFF_RV:_OFF_RV + 256],
                            w[:, _OFF_RG:_OFF_RG + 256]], axis=1).T


_P_DQ, _P_DK = 0, 512
_P_CQ, _P_CKV, _P_KR = 1024, 1280, 1408
_P_RQ, _P_RK = 1536, 1792
_P_COLS = 2048
_T_DV, _T_RV, _T_RG = 0, 512, 768
_T_ROWS = 1024


def _rope_lane_freqs():
    f64 = 1.0 / (ROPE_THETA ** (jnp.arange(0, 64, 2, dtype=F32) / 64))
    f32 = 1.0 / (ROPE_THETA ** (jnp.arange(0, 32, 2, dtype=F32) / 32))
    return jnp.concatenate([f64, f32, jnp.zeros((LANES - 48,), F32)])[None, :]


def _rope_table_kernel(pos_ref, invf_ref, c64_ref, s64_ref, cm_ref, sm_ref):
    a = pos_ref[...] * invf_ref[...]
    c, s = jnp.cos(a), jnp.sin(a)
    lane = lax.broadcasted_iota(jnp.int32, (1, LANES), 1)
    sign = jnp.where(lane < HALF, -1.0, 1.0)

    def spread(t, width, period):
        out = t
        for shift in range(period, LANES, period):
            out = jnp.where((lane >= shift) & (lane < shift + width), pltpu.roll(t, shift, 1), out)
        return out

    c64_ref[...] = spread(c, 32, 32)
    s64_ref[...] = spread(s, 32, 32) * sign
    rope = ((lane % HALF) >= 32) & ((lane % HALF) < 48)
    cm_ref[...] = jnp.where(rope, spread(c, 48, HALF), 1.0)
    sm_ref[...] = jnp.where(rope, spread(s, 48, HALF) * sign, 0.0)


def _rope_tables(pos, invf):
    m = pos.shape[0]
    bm = 2048
    tab = jax.ShapeDtypeStruct((m, LANES), F32)
    row = pl.BlockSpec((bm, LANES), lambda i: (i, 0))
    return pl.pallas_call(
        _rope_table_kernel,
        grid=(m // bm,),
        in_specs=[pl.BlockSpec((bm, 1), lambda i: (i, 0)),
                  pl.BlockSpec((1, LANES), lambda i: (0, 0))],
        out_specs=[row, row, row, row],
        out_shape=[tab, tab, tab, tab],
        name="rope_tables",
    )(pos, invf)


def _rms(t, gain, eps):
    return t * lax.rsqrt(jnp.mean(t * t, axis=-1, keepdims=True) + eps) * gain


def _rope_blocks(y, cos, sin):
    outs = []
    for j in range(y.shape[1] // LANES):
        blk = y[:, j * LANES:(j + 1) * LANES]
        outs.append(blk * cos + pltpu.roll(blk, HALF, 1) * sin)
    return outs[0] if len(outs) == 1 else jnp.concatenate(outs, axis=1)


def _pre_kernel(x_ref, g_ref, w_ref, wt_ref, c64_ref, s64_ref, cm_ref, sm_ref,
                qn_ref, wuq_ref, kvn_ref, wuk_ref, wuvt_ref,
                dqt_ref, dk_ref, dvt_ref, mqt_ref, mk_ref, mvt_ref,
                rq_ref, rk_ref, rvt_ref, rgt_ref, *, diff_qscale, mla_qscale, ret_kscale):
    hb = _rms(x_ref[...], g_ref[...], NORM_EPS).astype(BF16)

    def proj(off, n):
        return jnp.dot(hb, w_ref[:, off:off + n], preferred_element_type=F32)

    c64, s64 = c64_ref[...], s64_ref[...]
    cm, sm = cm_ref[...], sm_ref[...]

    dqt_ref[...] = (_rope_blocks(proj(_P_DQ, 512), c64, s64) * diff_qscale).T.astype(BF16)
    dk_ref[...] = _rope_blocks(proj(_P_DK, 512), c64, s64).astype(BF16)
    yt = lax.dot_general(wt_ref[...], hb, _NT, preferred_element_type=F32)
    dvt_ref[...] = yt[_T_DV:_T_DV + 512].astype(BF16)
    rvt_ref[...] = yt[_T_RV:_T_RV + 256].astype(BF16)
    rgt_ref[...] = yt[_T_RG:_T_RG + 256]

    cq = _rms(proj(_P_CQ, MLA_Q_RANK), qn_ref[...], NORM_EPS).astype(BF16)
    q = jnp.dot(cq, wuq_ref[...], preferred_element_type=F32)
    mqt_ref[...] = (_rope_blocks(q, cm, sm) * mla_qscale).T.astype(BF16)
    ckv = _rms(proj(_P_CKV, MLA_KV_RANK), kvn_ref[...], NORM_EPS).astype(BF16)
    kn = jnp.dot(ckv, wuk_ref[...], preferred_element_type=F32)
    kr = _rope_blocks(proj(_P_KR, LANES), cm, sm)
    mk_ref[...] = (kn + jnp.concatenate([kr] * MLA_HEADS, axis=1)).astype(BF16)
    mvt_ref[...] = lax.dot_general(wuvt_ref[...], ckv, _NT,
                                   preferred_element_type=F32).astype(BF16)

    rq_ref[...] = _rope_blocks(proj(_P_RQ, 256), c64, s64).astype(BF16)
    rk_ref[...] = (_rope_blocks(proj(_P_RK, 256), c64, s64) * ret_kscale).astype(BF16)


def _const_spec(shape):
    return pl.BlockSpec(shape, lambda i: (0,) * len(shape), pipeline_mode=pl.Buffered(1))


def _pre_attention(x, gain, w_all, wt_all, tabs, qn, wuq, kvn, wuk, wuvt):
    m = x.shape[0]
    bm = PRE_BM
    nb = m // bm
    row = lambda n: pl.BlockSpec((bm, n), lambda i: (i, 0))
    tr = lambda n: pl.BlockSpec((None, n, bm), lambda i: (i, 0, 0))
    out = lambda n, dt=BF16: jax.ShapeDtypeStruct((m, n), dt)
    out_t = lambda n, dt=BF16: jax.ShapeDtypeStruct((nb, n, bm), dt)
    kern = functools.partial(
        _pre_kernel,
        diff_qscale=DIFF_HEAD_DIM ** -0.5 * LOG2E,
        mla_qscale=(MLA_NOPE_DIM + MLA_ROPE_DIM) ** -0.5 * LOG2E,
        ret_kscale=RET_QK_DIM ** -0.5)
    return pl.pallas_call(
        kern,
        grid=(nb,),
        in_specs=[row(D_MODEL), _const_spec((1, D_MODEL)), _const_spec((D_MODEL, _P_COLS)),
                  _const_spec((_T_ROWS, D_MODEL)),
                  row(LANES), row(LANES), row(LANES), row(LANES),
                  _const_spec((1, MLA_Q_RANK)), _const_spec((MLA_Q_RANK, 512)),
                  _const_spec((1, MLA_KV_RANK)), _const_spec((MLA_KV_RANK, 512)),
                  _const_spec((256, MLA_KV_RANK))],
        out_specs=[tr(512), row(512), tr(512), tr(512), row(512), tr(256),
                   row(256), row(256), tr(256), tr(256)],
        out_shape=[out_t(512), out(512), out_t(512), out_t(512), out(512), out_t(256),
                   out(256), out(256), out_t(256), out_t(256, F32)],
        compiler_params=pltpu.CompilerParams(
            dimension_semantics=("parallel",), vmem_limit_bytes=VMEM_LIMIT),
        name="pre_attention",
    )(x, gain, w_all, wt_all, *tabs, qn, wuq, kvn, wuk, wuvt)


def _flash_kernel(*refs, diff, blk, sub, nblk, lam_init):
    if diff:
        qt_ref, k_ref, vt_ref, lam_ref, subln_ref, o_ref, s_sc, mx_sc, m_sc, l_sc, acc_sc = refs
    else:
        qt_ref, k_ref, vt_ref, o_ref, s_sc, mx_sc, m_sc, l_sc, acc_sc = refs
    nsub = blk // sub
    entry = 2

    if diff:
        feat = lax.broadcasted_iota(jnp.int32, (LANES, 1), 0)
        map_rows = [((feat % HALF) // 32) == t for t in range(2)]
        q_rows = [slice(0, LANES)] * 2
        k_lanes = [slice(0, LANES)] * 2
        v_rows = [slice(0, LANES)] * 2
    else:
        q_rows = [slice(0, LANES), slice(LANES, 2 * LANES)]
        k_lanes = [slice(0, LANES), slice(LANES, 2 * LANES)]
        v_rows = [slice(0, MLA_V_DIM), slice(MLA_V_DIM, 2 * MLA_V_DIM)]

    def scores(i, j, slot):
        start = pl.multiple_of(j * blk, blk)
        for t in range(2):
            kj = k_ref[pl.ds(start, blk), k_lanes[t]]
            for c in range(nsub):
                cols = slice(c * sub, (c + 1) * sub)
                qt = qt_ref[i, q_rows[t], cols]
                if diff:
                    qt = jnp.where(map_rows[t], qt, jnp.zeros_like(qt))
                s = jnp.dot(kj, qt, preferred_element_type=F32)
                s_sc[slot, t, c] = s
                mx_sc[slot, t, :, cols] = jnp.max(s, axis=0, keepdims=True)

    def accumulate(j, slot, diagonal):
        for t in range(2):
            for c in range(nsub):
                cols = slice(c * sub, (c + 1) * sub)
                nk = (c + 1) * sub if diagonal else blk
                s = s_sc[slot, t, c, 0:nk, :]
                if diagonal:
                    key = lax.broadcasted_iota(jnp.int32, (nk, sub), 0)
                    qry = lax.broadcasted_iota(jnp.int32, (nk, sub), 1) + c * sub
                    s = jnp.where(key <= qry, s, -jnp.inf)
                    m_cur = jnp.max(s, axis=0, keepdims=True)
                else:
                    m_cur = mx_sc[slot, t, :, cols]
                m_prev = m_sc[t, :, cols]
                m_new = jnp.maximum(m_prev, m_cur)
                alpha = jnp.exp2(m_prev - m_new)
                p = jnp.exp2(s - m_new)
                l_sc[t, :, cols] = alpha * l_sc[t, :, cols] + jnp.sum(p, axis=0, keepdims=True)
                vt = vt_ref[j, v_rows[t], 0:nk]
                acc_sc[t, c] = alpha * acc_sc[t, c] + jnp.dot(
                    vt, p.astype(BF16), preferred_element_type=F32)
                m_sc[t, :, cols] = m_new

    if diff:
        lv = lam_ref[...]
        lam = (jnp.exp(jnp.sum(lv[0:1] * lv[1:2], axis=1, keepdims=True))
               - jnp.exp(jnp.sum(lv[2:3] * lv[3:4], axis=1, keepdims=True)) + lam_init)

    def finalize(i):
        for c in range(nsub):
            cols = slice(c * sub, (c + 1) * sub)
            o0 = acc_sc[0, c] * (1.0 / l_sc[0, :, cols])
            o1 = acc_sc[1, c] * (1.0 / l_sc[1, :, cols])
            if diff:
                o = o0 - lam * o1
                ms = jnp.mean(o * o, axis=0, keepdims=True)
                o = o * lax.rsqrt(ms + DIFF_SUBLN_EPS) * subln_ref[:, cols] * (1.0 - lam_init)
            else:
                o = jnp.concatenate([o0, o1], axis=0)
            row = pl.multiple_of(i * blk + c * sub, sub)
            o_ref[pl.ds(row, sub), :] = o.T.astype(o_ref.dtype)

    scores(0, 0, entry)

    def query_block(i, carry):
        i_next = jnp.minimum(i + 1, nblk - 1)
        m_sc[...] = jnp.full(m_sc.shape, -jnp.inf, F32)
        l_sc[...] = jnp.zeros(l_sc.shape, F32)
        acc_sc[...] = jnp.zeros(acc_sc.shape, F32)

        @pl.when(i == 0)
        def _():
            accumulate(i, entry, True)
            scores(i_next, i_next, entry)

        @pl.when(i > 0)
        def _():
            scores(i, 0, 0)
            accumulate(i, entry, True)

        def pair(jj, c2):
            j0 = 2 * jj
            scores(i, j0 + 1, 1)
            accumulate(j0, 0, False)
            scores(i, j0 + 2, 0)
            accumulate(j0 + 1, 1, False)
            return c2

        lax.fori_loop(0, (i - 1) // 2, pair, 0)

        @pl.when(i % 2 == 1)
        def _():
            scores(i_next, i_next, entry)
            accumulate(i - 1, 0, False)

        @pl.when(jnp.logical_and(i % 2 == 0, i > 0))
        def _():
            scores(i, i - 1, 1)
            accumulate(i - 2, 0, False)
            scores(i_next, i_next, entry)
            accumulate(i - 1, 1, False)

        finalize(i)
        return carry

    lax.fori_loop(0, nblk, query_block, 0)


def _flash_attention(qt, k, vt, diff, lam=None, subln=None, lam_init=0.0):
    b, s, _ = k.shape
    blk, sub = ATT_BLOCK, ATT_SUB
    nblk, nsub = s // blk, blk // sub
    width = LANES if diff else 2 * LANES
    groups = k.shape[2] // width
    dv = LANES if diff else MLA_V_DIM
    qspec = pl.BlockSpec((None, nblk, width, blk), lambda bi, g: (bi, 0, g, 0))
    kspec = pl.BlockSpec((None, s, width), lambda bi, g: (bi, 0, g))
    vspec = pl.BlockSpec((None, nblk, LANES, blk), lambda bi, g: (bi, 0, g, 0))
    ospec = pl.BlockSpec((None, s, LANES), lambda bi, g: (bi, 0, g))
    in_specs = [qspec, kspec, vspec]
    args = [qt, k, vt]
    if diff:
        in_specs += [pl.BlockSpec((4, DIFF_HEAD_DIM), lambda bi, g: (0, 0)),
                     pl.BlockSpec((LANES, blk), lambda bi, g: (0, 0))]
        args += [lam, subln]
    return pl.pallas_call(
        functools.partial(_flash_kernel, diff=diff, blk=blk, sub=sub, nblk=nblk,
                          lam_init=lam_init),
        grid=(b, groups),
        in_specs=in_specs,
        out_specs=ospec,
        out_shape=jax.ShapeDtypeStruct((b, s, groups * LANES), BF16),
        scratch_shapes=[pltpu.VMEM((3, 2, nsub, blk, sub), F32),
                        pltpu.VMEM((3, 2, 1, blk), F32),
                        pltpu.VMEM((2, 1, blk), F32),
                        pltpu.VMEM((2, 1, blk), F32),
                        pltpu.VMEM((2, nsub, dv, sub), F32)],
        compiler_params=pltpu.CompilerParams(
            dimension_semantics=("parallel", "parallel"),
            vmem_limit_bytes=VMEM_LIMIT),
        name="diff_attention" if diff else "mla_attention",
    )(*args)


def _retention_consts():
    h, c = RET_HEADS, RET_CHUNK
    log_gamma = jnp.log1p(-jnp.exp2(-5.0 - jnp.arange(h, dtype=F32)))
    pos = jnp.arange(c, dtype=F32)
    rel = pos[:, None] - pos[None, :]
    decay = jnp.where(rel[None] >= 0,
                      jnp.exp(jnp.maximum(rel, 0.0)[None] * log_gamma[:, None, None]), 0.0)
    xi = jnp.exp((pos[None, :] + 1.0) * log_gamma[:, None])
    zeta = jnp.exp((c - 1.0 - pos)[None, :] * log_gamma[:, None])
    cdec = jnp.exp(c * log_gamma)
    decay_t = jnp.swapaxes(decay, 1, 2).astype(F32)
    xi_row = xi[:, None, :].astype(F32)
    zeta_col = jnp.broadcast_to(zeta[:, :, None], (h, c, LANES)).astype(F32)
    cdec_b = jnp.broadcast_to(cdec[:, None, None], (h, 1, LANES)).astype(F32)
    return decay_t, xi_row, zeta_col, cdec_b


def _retention_kernel(q_ref, k_ref, vt_ref, gt_ref, dec_ref, xi_ref, zeta_ref, cdec_ref,
                      gw_ref, gb_ref, o_ref, st_sc, *, rows):
    @pl.when(pl.program_id(1) == 0)
    def _():
        st_sc[...] = jnp.zeros(st_sc.shape, F32)

    lane = lax.broadcasted_iota(jnp.int32, (1, LANES), 1)
    pair_sel = [((lane % HALF) // 32) == hh for hh in range(2)]
    c = RET_CHUNK
    dv = RET_V_DIM
    for ci in range(rows // c):
        tok = slice(ci * c, (ci + 1) * c)
        for pv in range(RET_HEADS // 2):
            lanes = slice(pv * LANES, (pv + 1) * LANES)
            qp = q_ref[tok, lanes]
            kp = k_ref[tok, lanes]
            vtp = vt_ref[lanes, tok]
            halves = []
            for hh in range(2):
                h = 2 * pv + hh
                qh = jnp.where(pair_sel[hh], qp, jnp.zeros_like(qp))
                kh = jnp.where(pair_sel[hh], kp, jnp.zeros_like(kp))
                st = lax.dot_general(kh, qh, _NT, preferred_element_type=F32)
                inner = jnp.dot(vtp, (st * dec_ref[h]).astype(BF16),
                                preferred_element_type=F32)
                state = st_sc[h]
                cross = lax.dot_general(state.astype(BF16), qh, _NT,
                                        preferred_element_type=F32) * xi_ref[h]
                kz = (kh.astype(F32) * zeta_ref[h]).astype(BF16)
                st_sc[h] = (jnp.dot(vtp, kz, preferred_element_type=F32)
                            + cdec_ref[h] * state)
                o = (inner + cross)[hh * dv:(hh + 1) * dv]
                mu = jnp.mean(o, axis=0, keepdims=True)
                d = o - mu
                var = jnp.mean(d * d, axis=0, keepdims=True)
                halves.append(d * lax.rsqrt(var + RET_GN_EPS))
            on = jnp.concatenate(halves, axis=0) * gw_ref[lanes, :] + gb_ref[lanes, :]
            g = gt_ref[lanes, tok]
            o_ref[tok, lanes] = (g * jax.nn.sigmoid(g) * on).T.astype(o_ref.dtype)


def _retention(q, k, vt, gt, consts, gw, gb):
    b, s, w = q.shape
    rows = RET_ROWS
    c = RET_CHUNK
    blk = pl.BlockSpec((None, rows, w), lambda bi, i: (bi, i, 0))
    blk_t = pl.BlockSpec((None, None, w, rows), lambda bi, i: (bi, i, 0, 0))
    whole = lambda *shape: pl.BlockSpec(shape, lambda bi, i: (0,) * len(shape))
    return pl.pallas_call(
        functools.partial(_retention_kernel, rows=rows),
        grid=(b, s // rows),
        in_specs=[blk, blk, blk_t, blk_t,
                  whole(RET_HEADS, c, c), whole(RET_HEADS, 1, c),
                  whole(RET_HEADS, c, LANES), whole(RET_HEADS, 1, LANES),
                  whole(w, c), whole(w, c)],
        out_specs=blk,
        out_shape=jax.ShapeDtypeStruct((b, s, w), BF16),
        scratch_shapes=[pltpu.VMEM((RET_HEADS, LANES, LANES), F32)],
        compiler_params=pltpu.CompilerParams(
            dimension_semantics=("parallel", "arbitrary")),
        name="retention",
    )(q, k, vt, gt, *consts, gw, gb)


def _post_kernel(x_ref, od_ref, om_ref, or_ref, wod_ref, wom_ref, wor_ref, g_ref,
                 wg_ref, wu_ref, wd_ref, fg_ref, o_ref, a_sc, *, final):
    x1 = (x_ref[...]
          + jnp.dot(od_ref[...], wod_ref[...], preferred_element_type=F32)
          + jnp.dot(om_ref[...], wom_ref[...], preferred_element_type=F32)
          + jnp.dot(or_ref[...], wor_ref[...], preferred_element_type=F32))
    hb = _rms(x1, g_ref[...], NORM_EPS).astype(BF16)
    for c0 in range(0, FFN_HIDDEN, FFN_CHUNK):
        gate = jnp.dot(hb, wg_ref[:, c0:c0 + FFN_CHUNK], preferred_element_type=F32)
        up = jnp.dot(hb, wu_ref[:, c0:c0 + FFN_CHUNK], preferred_element_type=F32)
        a_sc[:, c0:c0 + FFN_CHUNK] = (gate * jax.nn.sigmoid(gate) * up).astype(BF16)
    y = x1 + jnp.dot(a_sc[...], wd_ref[...], preferred_element_type=F32)
    if final:
        y = _rms(y, fg_ref[...], NORM_EPS)
    o_ref[...] = y


def _post_attention(x, od, om, orr, wod, wom, wor, gain, wg, wu, wd, fgain, final):
    m = x.shape[0]
    bm = POST_BM
    row = lambda n: pl.BlockSpec((bm, n), lambda i: (i, 0))
    return pl.pallas_call(
        functools.partial(_post_kernel, final=final),
        grid=(m // bm,),
        in_specs=[row(D_MODEL), row(512), row(256), row(256),
                  _const_spec((512, D_MODEL)), _const_spec((256, D_MODEL)),
                  _const_spec((256, D_MODEL)), _const_spec((1, D_MODEL)),
                  _const_spec((D_MODEL, FFN_HIDDEN)), _const_spec((D_MODEL, FFN_HIDDEN)),
                  _const_spec((FFN_HIDDEN, D_MODEL)), _const_spec((1, D_MODEL))],
        out_specs=row(D_MODEL),
        out_shape=jax.ShapeDtypeStruct((m, D_MODEL), F32),
        scratch_shapes=[pltpu.VMEM((bm, FFN_HIDDEN), BF16)],
        compiler_params=pltpu.CompilerParams(
            dimension_semantics=("parallel",), vmem_limit_bytes=VMEM_LIMIT),
        name="post_attention",
    )(x, od, om, orr, wod, wom, wor, gain, wg, wu, wd, fgain)


def kernel(x, positions, attn_norm, w_in, diff_lam_q1, diff_lam_k1, diff_lam_q2, diff_lam_k2, diff_subln, mla_q_norm, mla_w_uq, mla_kv_norm, mla_w_ukv, ret_gn_w, ret_gn_b, w_out, ffn_norm, w_gate, w_up, w_down, final_norm):
    b, s, d = x.shape
    m = b * s
    nblk = s // ATT_BLOCK
    tabs = _rope_tables(positions.astype(F32).reshape(m, 1), _rope_lane_freqs())
    ret_consts = _retention_consts()

    xf = x.reshape(m, d)
    for layer in range(DEPTH):
        lam_init = 0.8 - 0.6 * math.exp(-0.3 * layer)
        w_all = _in_proj_layout(w_in[layer]).astype(BF16)
        wt_all = _in_proj_layout_t(w_in[layer]).astype(BF16)
        wuq = _mla_q_layout(mla_w_uq[layer]).astype(BF16)
        wuk, wuv = _mla_kv_layout(mla_w_ukv[layer])
        (dqt, dk, dvt, mqt, mk, mvt, rq, rk, rvt, rgt) = _pre_attention(
            xf, attn_norm[layer][None], w_all, wt_all, tabs,
            mla_q_norm[layer][None], wuq, mla_kv_norm[layer][None],
            wuk.astype(BF16), wuv.T.astype(BF16))

        sh = lambda t: t.reshape(b, s, t.shape[-1])
        sh_t = lambda t: t.reshape(b, nblk, t.shape[1], ATT_BLOCK)
        lam = jnp.stack([diff_lam_q1[layer], diff_lam_k1[layer],
                         diff_lam_q2[layer], diff_lam_k2[layer]])
        subln = jnp.broadcast_to(diff_subln[layer][:, None], (LANES, ATT_BLOCK))
        o_diff = _flash_attention(sh_t(dqt), sh(dk), sh_t(dvt), True, lam=lam,
                                  subln=subln, lam_init=lam_init)
        o_mla = _flash_attention(sh_t(mqt), sh(mk), sh_t(mvt), False)
        gn_col = lambda t: jnp.broadcast_to(t[:, None], (t.shape[0], RET_CHUNK))
        o_ret = _retention(sh(rq), sh(rk), sh_t(rvt), sh_t(rgt), ret_consts,
                           gn_col(ret_gn_w[layer]), gn_col(ret_gn_b[layer]))

        wo = w_out[layer].astype(BF16)
        xf = _post_attention(
            xf, o_diff.reshape(m, 512), o_mla.reshape(m, 256), o_ret.reshape(m, 256),
            wo[:512], wo[512:768], wo[768:], ffn_norm[layer][None],
            w_gate[layer].astype(BF16), w_up[layer].astype(BF16), w_down[layer].astype(BF16),
            final_norm[None], final=(layer == DEPTH - 1))
    return xf.reshape(b, s, d)
```

```python
import functools
import math

import jax
import jax.numpy as jnp
from jax import lax
from jax.experimental import pallas as pl
from jax.experimental.pallas import tpu as pltpu

F32 = jnp.float32
BF16 = jnp.bfloat16

D_MODEL = 1024
DEPTH = 2
ROPE_THETA = 10000.0
NORM_EPS = 1e-6

DIFF_HEADS = 4
DIFF_HEAD_DIM = 64
DIFF_SUBLN_EPS = 1e-5
MLA_HEADS = 4
MLA_NOPE_DIM = 64
MLA_ROPE_DIM = 32
MLA_V_DIM = 64
MLA_Q_RANK = 256
MLA_KV_RANK = 128
RET_HEADS = 4
RET_QK_DIM = 64
RET_V_DIM = 64
RET_CHUNK = 128
RET_GN_EPS = 1e-6
FFN_HIDDEN = 2816

LANES = 128
HALF = LANES // 2
LOG2E = math.log2(math.e)

_OFF_DQ, _OFF_DK, _OFF_DV = 0, 512, 1024
_OFF_CQ, _OFF_CKV, _OFF_KR = 1536, 1792, 1920
_OFF_RQ, _OFF_RK, _OFF_RV, _OFF_RG = 1952, 2208, 2464, 2720

ATT_BLOCK = 512
ATT_SUB = 256
PRE_BM = ATT_BLOCK
POST_BM = 512
RET_ROWS = 512
FFN_CHUNK = 256
VMEM_LIMIT = 56 * 1024 * 1024

_NT = (((1,), (1,)), ((), ()))


def _pair_layout(w):
    *lead, n = w.shape
    return w.reshape(*lead, n // LANES, 2, 2, 32).swapaxes(-3, -2).reshape(*lead, n)


def _mla_q_layout(w):
    lead = w.shape[:-1]
    w = w.reshape(*lead, MLA_HEADS, MLA_NOPE_DIM + MLA_ROPE_DIM)
    nope = w[..., :MLA_NOPE_DIM].reshape(*lead, MLA_HEADS, 2, 32)
    rope = w[..., MLA_NOPE_DIM:].reshape(*lead, MLA_HEADS, 2, 16)
    pad = jnp.zeros((*lead, MLA_HEADS, 2, 16), w.dtype)
    return jnp.concatenate([nope, rope, pad], axis=-1).reshape(*lead, MLA_HEADS * LANES)


def _mla_kv_layout(w):
    lead = w.shape[:-1]
    w = w.reshape(*lead, MLA_HEADS, MLA_NOPE_DIM + MLA_V_DIM)
    nope = w[..., :MLA_NOPE_DIM].reshape(*lead, MLA_HEADS, 2, 32)
    pad = jnp.zeros((*lead, MLA_HEADS, 2, 32), w.dtype)
    wk = jnp.concatenate([nope, pad], axis=-1).reshape(*lead, MLA_HEADS * LANES)
    wv = w[..., MLA_NOPE_DIM:].reshape(*lead, MLA_HEADS * MLA_V_DIM)
    return wk, wv


def _mla_krope_layout(w):
    lead = w.shape[:-1]
    kr = w.reshape(*lead, 2, 16)
    return jnp.concatenate([jnp.zeros((*lead, 2, 32), w.dtype), kr,
                            jnp.zeros((*lead, 2, 16), w.dtype)], axis=-1).reshape(*lead, LANES)


def _in_proj_layout(w):
    sl = lambda off, n: w[..., off:off + n]
    return jnp.concatenate([
        _pair_layout(sl(_OFF_DQ, 512)), _pair_layout(sl(_OFF_DK, 512)),
        sl(_OFF_CQ, MLA_Q_RANK), sl(_OFF_CKV, MLA_KV_RANK),
        _mla_krope_layout(sl(_OFF_KR, MLA_ROPE_DIM)),
        _pair_layout(sl(_OFF_RQ, 256)), _pair_layout(sl(_OFF_RK, 256))], axis=-1)


def _in_proj_layout_t(w):
    return jnp.concatenate([w[..., _OFF_DV:_OFF_DV + 512], w[..., _OFF_RV:_OFF_RV + 256],
                            w[..., _OFF_RG:_OFF_RG + 256]], axis=-1).swapaxes(-1, -2)


_P_DQ, _P_DK = 0, 512
_P_CQ, _P_CKV, _P_KR = 1024, 1280, 1408
_P_RQ, _P_RK = 1536, 1792
_P_COLS = 2048
_T_DV, _T_RV, _T_RG = 0, 512, 768
_T_ROWS = 1024


def _rope_lane_freqs():
    f64 = 1.0 / (ROPE_THETA ** (jnp.arange(0, 64, 2, dtype=F32) / 64))
    f32 = 1.0 / (ROPE_THETA ** (jnp.arange(0, 32, 2, dtype=F32) / 32))
    return jnp.concatenate([f64, f32, jnp.zeros((LANES - 48,), F32)])[None, :]


def _rope_table_kernel(pos_ref, invf_ref, c64_ref, s64_ref, cm_ref, sm_ref):
    a = pos_ref[...] * invf_ref[...]
    c, s = jnp.cos(a), jnp.sin(a)
    lane = lax.broadcasted_iota(jnp.int32, (1, LANES), 1)
    sign = jnp.where(lane < HALF, -1.0, 1.0)

    def spread(t, width, period):
        out = t
        for shift in range(period, LANES, period):
            out = jnp.where((lane >= shift) & (lane < shift + width), pltpu.roll(t, shift, 1), out)
        return out

    c64_ref[...] = spread(c, 32, 32)
    s64_ref[...] = spread(s, 32, 32) * sign
    rope = ((lane % HALF) >= 32) & ((lane % HALF) < 48)
    cm_ref[...] = jnp.where(rope, spread(c, 48, HALF), 1.0)
    sm_ref[...] = jnp.where(rope, spread(s, 48, HALF) * sign, 0.0)


def _rope_tables(pos, invf):
    m = pos.shape[0]
    bm = 2048
    tab = jax.ShapeDtypeStruct((m, LANES), F32)
    row = pl.BlockSpec((bm, LANES), lambda i: (i, 0))
    return pl.pallas_call(
        _rope_table_kernel,
        grid=(m // bm,),
        in_specs=[pl.BlockSpec((bm, 1), lambda i: (i, 0)),
                  pl.BlockSpec((1, LANES), lambda i: (0, 0))],
        out_specs=[row, row, row, row],
        out_shape=[tab, tab, tab, tab],
        name="rope_tables",
    )(pos, invf)


def _rms(t, gain, eps):
    return t * lax.rsqrt(jnp.mean(t * t, axis=-1, keepdims=True) + eps) * gain


def _rope_blocks(y, cos, sin):
    outs = []
    for j in range(y.shape[1] // LANES):
        blk = y[:, j * LANES:(j + 1) * LANES]
        outs.append(blk * cos + pltpu.roll(blk, HALF, 1) * sin)
    return outs[0] if len(outs) == 1 else jnp.concatenate(outs, axis=1)


def _pre_kernel(x_ref, g_ref, w_ref, wt_ref, c64_ref, s64_ref, cm_ref, sm_ref,
                qn_ref, wuq_ref, kvn_ref, wuk_ref, wuvt_ref,
                dqt_ref, dk_ref, dvt_ref, mqt_ref, mk_ref, mvt_ref,
                rq_ref, rk_ref, rvt_ref, rgt_ref, *, diff_qscale, mla_qscale, ret_kscale):
    hb = _rms(x_ref[...], g_ref[...], NORM_EPS).astype(BF16)

    def proj(off, n):
        return jnp.dot(hb, w_ref[:, off:off + n], preferred_element_type=F32)

    c64, s64 = c64_ref[...], s64_ref[...]
    cm, sm = cm_ref[...], sm_ref[...]

    dqt_ref[...] = (_rope_blocks(proj(_P_DQ, 512), c64, s64) * diff_qscale).T.astype(BF16)
    dk_ref[...] = _rope_blocks(proj(_P_DK, 512), c64, s64).astype(BF16)
    yt = lax.dot_general(wt_ref[...], hb, _NT, preferred_element_type=F32)
    dvt_ref[...] = yt[_T_DV:_T_DV + 512].astype(BF16)
    rvt_ref[...] = yt[_T_RV:_T_RV + 256].astype(BF16)
    rgt_ref[...] = yt[_T_RG:_T_RG + 256]

    cq = _rms(proj(_P_CQ, MLA_Q_RANK), qn_ref[...], NORM_EPS).astype(BF16)
    q = jnp.dot(cq, wuq_ref[...], preferred_element_type=F32)
    mqt_ref[...] = (_rope_blocks(q, cm, sm) * mla_qscale).T.astype(BF16)
    ckv = _rms(proj(_P_CKV, MLA_KV_RANK), kvn_ref[...], NORM_EPS).astype(BF16)
    kn = jnp.dot(ckv, wuk_ref[...], preferred_element_type=F32)
    kr = _rope_blocks(proj(_P_KR, LANES), cm, sm)
    mk_ref[...] = (kn + jnp.concatenate([kr] * MLA_HEADS, axis=1)).astype(BF16)
    mvt_ref[...] = lax.dot_general(wuvt_ref[...], ckv, _NT,
                                   preferred_element_type=F32).astype(BF16)

    rq_ref[...] = _rope_blocks(proj(_P_RQ, 256), c64, s64).astype(BF16)
    rk_ref[...] = (_rope_blocks(proj(_P_RK, 256), c64, s64) * ret_kscale).astype(BF16)


def _layer_spec(shape, layer):
    return pl.BlockSpec((None, *shape), lambda *_: (layer,) + (0,) * len(shape),
                        pipeline_mode=pl.Buffered(1))


def _pre_attention(layer, x, gain, w_all, wt_all, tabs, qn, wuq, kvn, wuk, wuvt):
    _const_spec = functools.partial(_layer_spec, layer=layer)
    m = x.shape[0]
    bm = PRE_BM
    nb = m // bm
    row = lambda n: pl.BlockSpec((bm, n), lambda i: (i, 0))
    tr = lambda n: pl.BlockSpec((None, n, bm), lambda i: (i, 0, 0))
    out = lambda n, dt=BF16: jax.ShapeDtypeStruct((m, n), dt)
    out_t = lambda n, dt=BF16: jax.ShapeDtypeStruct((nb, n, bm), dt)
    kern = functools.partial(
        _pre_kernel,
        diff_qscale=DIFF_HEAD_DIM ** -0.5 * LOG2E,
        mla_qscale=(MLA_NOPE_DIM + MLA_ROPE_DIM) ** -0.5 * LOG2E,
        ret_kscale=RET_QK_DIM ** -0.5)
    return pl.pallas_call(
        kern,
        grid=(nb,),
        in_specs=[row(D_MODEL), _const_spec((1, D_MODEL)), _const_spec((D_MODEL, _P_COLS)),
                  _const_spec((_T_ROWS, D_MODEL)),
                  row(LANES), row(LANES), row(LANES), row(LANES),
                  _const_spec((1, MLA_Q_RANK)), _const_spec((MLA_Q_RANK, 512)),
                  _const_spec((1, MLA_KV_RANK)), _const_spec((MLA_KV_RANK, 512)),
                  _const_spec((256, MLA_KV_RANK))],
        out_specs=[tr(512), row(512), tr(512), tr(512), row(512), tr(256),
                   row(256), row(256), tr(256), tr(256)],
        out_shape=[out_t(512), out(512), out_t(512), out_t(512), out(512), out_t(256),
                   out(256), out(256), out_t(256), out_t(256, F32)],
        compiler_params=pltpu.CompilerParams(
            dimension_semantics=("parallel",), vmem_limit_bytes=VMEM_LIMIT),
        name="pre_attention",
    )(x, gain, w_all, wt_all, *tabs, qn, wuq, kvn, wuk, wuvt)


def _flash_kernel(*refs, diff, blk, sub, nblk, lam_init):
    if diff:
        qt_ref, k_ref, vt_ref, lam_ref, subln_ref, o_ref, s_sc, mx_sc, m_sc, l_sc, acc_sc = refs
    else:
        qt_ref, k_ref, vt_ref, o_ref, s_sc, mx_sc, m_sc, l_sc, acc_sc = refs
    nsub = blk // sub
    entry = 2

    if diff:
        feat = lax.broadcasted_iota(jnp.int32, (LANES, 1), 0)
        map_rows = [((feat % HALF) // 32) == t for t in range(2)]
        q_rows = [slice(0, LANES)] * 2
        k_lanes = [slice(0, LANES)] * 2
        v_rows = [slice(0, LANES)] * 2
    else:
        q_rows = [slice(0, LANES), slice(LANES, 2 * LANES)]
        k_lanes = [slice(0, LANES), slice(LANES, 2 * LANES)]
        v_rows = [slice(0, MLA_V_DIM), slice(MLA_V_DIM, 2 * MLA_V_DIM)]

    def scores(i, j, slot, diagonal=False):
        start = pl.multiple_of(j * blk, blk)
        for t in range(2):
            for c in range(nsub):
                cols = slice(c * sub, (c + 1) * sub)
                nk = (c + 1) * sub if diagonal else blk
                kj = k_ref[pl.ds(start, nk), k_lanes[t]]
                qt = qt_ref[i, q_rows[t], cols]
                if diff:
                    qt = jnp.where(map_rows[t], qt, jnp.zeros_like(qt))
                s = jnp.dot(kj, qt, preferred_element_type=F32)
                s_sc[slot, t, c, 0:nk, :] = s
                if not diagonal:
                    mx_sc[slot, t, :, cols] = jnp.max(s, axis=0, keepdims=True)

    def accumulate(j, slot, diagonal):
        for t in range(2):
            for c in range(nsub):
                cols = slice(c * sub, (c + 1) * sub)
                nk = (c + 1) * sub if diagonal else blk
                s = s_sc[slot, t, c, 0:nk, :]
                if diagonal:
                    key = lax.broadcasted_iota(jnp.int32, (nk, sub), 0)
                    qry = lax.broadcasted_iota(jnp.int32, (nk, sub), 1) + c * sub
                    s = jnp.where(key <= qry, s, -jnp.inf)
                    m_cur = jnp.max(s, axis=0, keepdims=True)
                else:
                    m_cur = mx_sc[slot, t, :, cols]
                m_prev = m_sc[t, :, cols]
                m_new = jnp.maximum(m_prev, m_cur)
                alpha = jnp.exp2(m_prev - m_new)
                p = jnp.exp2(s - m_new)
                l_sc[t, :, cols] = alpha * l_sc[t, :, cols] + jnp.sum(p, axis=0, keepdims=True)
                vt = vt_ref[j, v_rows[t], 0:nk]
                acc_sc[t, c] = alpha * acc_sc[t, c] + jnp.dot(
                    vt, p.astype(BF16), preferred_element_type=F32)
                m_sc[t, :, cols] = m_new

    if diff:
        lv = lam_ref[...]
        lam = (jnp.exp(jnp.sum(lv[0:1] * lv[1:2], axis=1, keepdims=True))
               - jnp.exp(jnp.sum(lv[2:3] * lv[3:4], axis=1, keepdims=True)) + lam_init)

    def finalize(i):
        for c in range(nsub):
            cols = slice(c * sub, (c + 1) * sub)
            o0 = acc_sc[0, c] * (1.0 / l_sc[0, :, cols])
            o1 = acc_sc[1, c] * (1.0 / l_sc[1, :, cols])
            if diff:
                o = o0 - lam * o1
                ms = jnp.mean(o * o, axis=0, keepdims=True)
                o = o * lax.rsqrt(ms + DIFF_SUBLN_EPS) * subln_ref[:, cols] * (1.0 - lam_init)
            else:
                o = jnp.concatenate([o0, o1], axis=0)
            row = pl.multiple_of(i * blk + c * sub, sub)
            o_ref[pl.ds(row, sub), :] = o.T.astype(o_ref.dtype)

    scores(0, 0, entry, diagonal=True)

    def query_block(i, carry):
        i_next = jnp.minimum(i + 1, nblk - 1)
        m_sc[...] = jnp.full(m_sc.shape, -jnp.inf, F32)
        l_sc[...] = jnp.zeros(l_sc.shape, F32)
        acc_sc[...] = jnp.zeros(acc_sc.shape, F32)

        @pl.when(i == 0)
        def _():
            accumulate(i, entry, True)
            scores(i_next, i_next, entry, diagonal=True)

        @pl.when(i > 0)
        def _():
            scores(i, 0, 0)
            accumulate(i, entry, True)

        def pair(jj, c2):
            j0 = 2 * jj
            scores(i, j0 + 1, 1)
            accumulate(j0, 0, False)
            scores(i, j0 + 2, 0)
            accumulate(j0 + 1, 1, False)
            return c2

        npairs = jnp.maximum(i - 1, 0) // 2

        def quad(qq, c2):
            pair(2 * qq, c2)
            return pair(2 * qq + 1, c2)

        lax.fori_loop(0, npairs // 2, quad, 0)

        @pl.when(npairs % 2 == 1)
        def _():
            pair(npairs - 1, 0)

        @pl.when(i % 2 == 1)
        def _():
            scores(i_next, i_next, entry, diagonal=True)
            accumulate(i - 1, 0, False)

        @pl.when(jnp.logical_and(i % 2 == 0, i > 0))
        def _():
            scores(i, i - 1, 1)
            accumulate(i - 2, 0, False)
            scores(i_next, i_next, entry, diagonal=True)
            accumulate(i - 1, 1, False)

        finalize(i)
        return carry

    lax.fori_loop(0, nblk, query_block, 0)


def _flash_attention(qt, k, vt, diff, layer=0, lam=None, subln=None, lam_init=0.0):
    b, s, _ = k.shape
    blk, sub = ATT_BLOCK, ATT_SUB
    nblk, nsub = s // blk, blk // sub
    width = LANES if diff else 2 * LANES
    groups = k.shape[2] // width
    dv = LANES if diff else MLA_V_DIM
    qspec = pl.BlockSpec((None, nblk, width, blk), lambda bi, g: (bi, 0, g, 0))
    kspec = pl.BlockSpec((None, s, width), lambda bi, g: (bi, 0, g))
    vspec = pl.BlockSpec((None, nblk, LANES, blk), lambda bi, g: (bi, 0, g, 0))
    ospec = pl.BlockSpec((None, s, LANES), lambda bi, g: (bi, 0, g))
    in_specs = [qspec, kspec, vspec]
    args = [qt, k, vt]
    if diff:
        in_specs += [_layer_spec((4, DIFF_HEAD_DIM), layer), _layer_spec((LANES, blk), layer)]
        args += [lam, subln]
    return pl.pallas_call(
        functools.partial(_flash_kernel, diff=diff, blk=blk, sub=sub, nblk=nblk,
                          lam_init=lam_init),
        grid=(b, groups),
        in_specs=in_specs,
        out_specs=ospec,
        out_shape=jax.ShapeDtypeStruct((b, s, groups * LANES), BF16),
        scratch_shapes=[pltpu.VMEM((3, 2, nsub, blk, sub), F32),
                        pltpu.VMEM((3, 2, 1, blk), F32),
                        pltpu.VMEM((2, 1, blk), F32),
                        pltpu.VMEM((2, 1, blk), F32),
                        pltpu.VMEM((2, nsub, dv, sub), F32)],
        compiler_params=pltpu.CompilerParams(
            dimension_semantics=("parallel", "parallel"),
            vmem_limit_bytes=VMEM_LIMIT),
        name="diff_attention" if diff else "mla_attention",
    )(*args)


def _retention_consts():
    h, c = RET_HEADS, RET_CHUNK
    log_gamma = jnp.log1p(-jnp.exp2(-5.0 - jnp.arange(h, dtype=F32)))
    pos = jnp.arange(c, dtype=F32)
    rel = pos[:, None] - pos[None, :]
    decay = jnp.where(rel[None] >= 0,
                      jnp.exp(jnp.maximum(rel, 0.0)[None] * log_gamma[:, None, None]), 0.0)
    xi = jnp.exp((pos[None, :] + 1.0) * log_gamma[:, None])
    zeta = jnp.exp((c - 1.0 - pos)[None, :] * log_gamma[:, None])
    cdec = jnp.exp(c * log_gamma)
    decay_t = jnp.swapaxes(decay, 1, 2).astype(F32)
    xi_row = xi[:, None, :].astype(F32)
    zeta_col = jnp.broadcast_to(zeta[:, :, None], (h, c, LANES)).astype(F32)
    cdec_b = jnp.broadcast_to(cdec[:, None, None], (h, 1, LANES)).astype(F32)
    return decay_t, xi_row, zeta_col, cdec_b


def _retention_kernel(q_ref, k_ref, vt_ref, gt_ref, dec_ref, xi_ref, zeta_ref, cdec_ref,
                      gw_ref, gb_ref, o_ref, st_sc, *, rows):
    @pl.when(pl.program_id(1) == 0)
    def _():
        st_sc[...] = jnp.zeros(st_sc.shape, F32)

    lane = lax.broadcasted_iota(jnp.int32, (1, LANES), 1)
    pair_sel = [((lane % HALF) // 32) == hh for hh in range(2)]
    c = RET_CHUNK
    dv = RET_V_DIM
    for ci in range(rows // c):
        tok = slice(ci * c, (ci + 1) * c)
        for pv in range(RET_HEADS // 2):
            lanes = slice(pv * LANES, (pv + 1) * LANES)
            qp = q_ref[tok, lanes]
            kp = k_ref[tok, lanes]
            vtp = vt_ref[lanes, tok]
            halves = []
            for hh in range(2):
                h = 2 * pv + hh
                qh = jnp.where(pair_sel[hh], qp, jnp.zeros_like(qp))
                kh = jnp.where(pair_sel[hh], kp, jnp.zeros_like(kp))
                st = lax.dot_general(kh, qh, _NT, preferred_element_type=F32)
                inner = jnp.dot(vtp, (st * dec_ref[h]).astype(BF16),
                                preferred_element_type=F32)
                state = st_sc[h]
                cross = lax.dot_general(state.astype(BF16), qh, _NT,
                                        preferred_element_type=F32) * xi_ref[h]
                kz = (kh.astype(F32) * zeta_ref[h]).astype(BF16)
                st_sc[h] = (jnp.dot(vtp, kz, preferred_element_type=F32)
                            + cdec_ref[h] * state)
                o = (inner + cross)[hh * dv:(hh + 1) * dv]
                mu = jnp.mean(o, axis=0, keepdims=True)
                d = o - mu
                var = jnp.mean(d * d, axis=0, keepdims=True)
                halves.append(d * lax.rsqrt(var + RET_GN_EPS))
            on = jnp.concatenate(halves, axis=0) * gw_ref[lanes, :] + gb_ref[lanes, :]
            g = gt_ref[lanes, tok]
            o_ref[tok, lanes] = (g * jax.nn.sigmoid(g) * on).T.astype(o_ref.dtype)


def _retention(layer, q, k, vt, gt, consts, gw, gb):
    b, s, w = q.shape
    rows = RET_ROWS
    c = RET_CHUNK
    blk = pl.BlockSpec((None, rows, w), lambda bi, i: (bi, i, 0))
    blk_t = pl.BlockSpec((None, None, w, rows), lambda bi, i: (bi, i, 0, 0))
    whole = lambda *shape: pl.BlockSpec(shape, lambda bi, i: (0,) * len(shape))
    return pl.pallas_call(
        functools.partial(_retention_kernel, rows=rows),
        grid=(b, s // rows),
        in_specs=[blk, blk, blk_t, blk_t,
                  whole(RET_HEADS, c, c), whole(RET_HEADS, 1, c),
                  whole(RET_HEADS, c, LANES), whole(RET_HEADS, 1, LANES),
                  _layer_spec((w, c), layer), _layer_spec((w, c), layer)],
        out_specs=blk,
        out_shape=jax.ShapeDtypeStruct((b, s, w), BF16),
        scratch_shapes=[pltpu.VMEM((RET_HEADS, LANES, LANES), F32)],
        compiler_params=pltpu.CompilerParams(
            dimension_semantics=("parallel", "arbitrary")),
        name="retention",
    )(q, k, vt, gt, *consts, gw, gb)


def _post_kernel(x_ref, od_ref, om_ref, or_ref, wo_ref, g_ref,
                 wg_ref, wu_ref, wd_ref, fg_ref, o_ref, a_sc, *, final):
    x1 = (x_ref[...]
          + jnp.dot(od_ref[...], wo_ref[0:512, :], preferred_element_type=F32)
          + jnp.dot(om_ref[...], wo_ref[512:768, :], preferred_element_type=F32)
          + jnp.dot(or_ref[...], wo_ref[768:1024, :], preferred_element_type=F32))
    hb = _rms(x1, g_ref[...], NORM_EPS).astype(BF16)
    for c0 in range(0, FFN_HIDDEN, FFN_CHUNK):
        gate = jnp.dot(hb, wg_ref[:, c0:c0 + FFN_CHUNK], preferred_element_type=F32)
        up = jnp.dot(hb, wu_ref[:, c0:c0 + FFN_CHUNK], preferred_element_type=F32)
        a_sc[:, c0:c0 + FFN_CHUNK] = (gate * jax.nn.sigmoid(gate) * up).astype(BF16)
    y = x1 + jnp.dot(a_sc[...], wd_ref[...], preferred_element_type=F32)
    if final:
        y = _rms(y, fg_ref[...], NORM_EPS)
    o_ref[...] = y


def _post_attention(layer, x, od, om, orr, wo, gain, wg, wu, wd, fgain, final):
    _const_spec = functools.partial(_layer_spec, layer=layer)
    m = x.shape[0]
    bm = POST_BM
    row = lambda n: pl.BlockSpec((bm, n), lambda i: (i, 0))
    return pl.pallas_call(
        functools.partial(_post_kernel, final=final),
        grid=(m // bm,),
        in_specs=[row(D_MODEL), row(512), row(256), row(256),
                  _const_spec((D_MODEL, D_MODEL)), _const_spec((1, D_MODEL)),
                  _const_spec((D_MODEL, FFN_HIDDEN)), _const_spec((D_MODEL, FFN_HIDDEN)),
                  _const_spec((FFN_HIDDEN, D_MODEL)),
                  pl.BlockSpec((1, D_MODEL), lambda i: (0, 0))],
        out_specs=row(D_MODEL),
        out_shape=jax.ShapeDtypeStruct((m, D_MODEL), F32),
        scratch_shapes=[pltpu.VMEM((bm, FFN_HIDDEN), BF16)],
        compiler_params=pltpu.CompilerParams(
            dimension_semantics=("parallel",), vmem_limit_bytes=VMEM_LIMIT),
        name="post_attention",
    )(x, od, om, orr, wo, gain, wg, wu, wd, fgain)


def kernel(x, positions, attn_norm, w_in, diff_lam_q1, diff_lam_k1, diff_lam_q2, diff_lam_k2, diff_subln, mla_q_norm, mla_w_uq, mla_kv_norm, mla_w_ukv, ret_gn_w, ret_gn_b, w_out, ffn_norm, w_gate, w_up, w_down, final_norm):
    b, s, d = x.shape
    m = b * s
    nblk = s // ATT_BLOCK
    tabs = _rope_tables(positions.astype(F32).reshape(m, 1), _rope_lane_freqs())
    ret_consts = _retention_consts()

    row_vec = lambda t: t[:, None, :]
    w_all = _in_proj_layout(w_in).astype(BF16)
    wt_all = _in_proj_layout_t(w_in).astype(BF16)
    wuq = _mla_q_layout(mla_w_uq).astype(BF16)
    wuk, wuv = _mla_kv_layout(mla_w_ukv)
    wuk, wuvt = wuk.astype(BF16), wuv.swapaxes(-1, -2).astype(BF16)
    lam = jnp.stack([diff_lam_q1, diff_lam_k1, diff_lam_q2, diff_lam_k2], axis=1)
    subln = jnp.broadcast_to(diff_subln[:, :, None], (DEPTH, LANES, ATT_BLOCK))
    gn_col = lambda t: jnp.broadcast_to(t[:, :, None], (*t.shape, RET_CHUNK))
    gn_w, gn_b = gn_col(ret_gn_w), gn_col(ret_gn_b)
    wo, wg, wu, wd = (t.astype(BF16) for t in (w_out, w_gate, w_up, w_down))
    attn_g, ffn_g, qn_g, kvn_g = (row_vec(t) for t in (attn_norm, ffn_norm, mla_q_norm, mla_kv_norm))

    sh = lambda t: t.reshape(b, s, t.shape[-1])
    sh_t = lambda t: t.reshape(b, nblk, t.shape[1], ATT_BLOCK)
    xf = x.reshape(m, d)
    for layer in range(DEPTH):
        lam_init = 0.8 - 0.6 * math.exp(-0.3 * layer)
        (dqt, dk, dvt, mqt, mk, mvt, rq, rk, rvt, rgt) = _pre_attention(
            layer, xf, attn_g, w_all, wt_all, tabs, qn_g, wuq, kvn_g, wuk, wuvt)
        o_diff = _flash_attention(sh_t(dqt), sh(dk), sh_t(dvt), True, layer=layer, lam=lam,
                                  subln=subln, lam_init=lam_init)
        o_mla = _flash_attention(sh_t(mqt), sh(mk), sh_t(mvt), False)
        o_ret = _retention(layer, sh(rq), sh(rk), sh_t(rvt), sh_t(rgt), ret_consts, gn_w, gn_b)
        xf = _post_attention(
            layer, xf, o_diff.reshape(m, 512), o_mla.reshape(m, 256), o_ret.reshape(m, 256),
            wo, ffn_g, wg, wu, wd, final_norm[None], final=(layer == DEPTH - 1))
    return xf.reshape(b, s, d)
```

```python
import functools
import math

import jax
import jax.numpy as jnp
from jax import lax
from jax.experimental import pallas as pl
from jax.experimental.pallas import tpu as pltpu

F32 = jnp.float32
BF16 = jnp.bfloat16

D_MODEL = 1024
DEPTH = 2
ROPE_THETA = 10000.0
NORM_EPS = 1e-6

DIFF_HEADS = 4
DIFF_HEAD_DIM = 64
DIFF_SUBLN_EPS = 1e-5
MLA_HEADS = 4
MLA_NOPE_DIM = 64
MLA_ROPE_DIM = 32
MLA_V_DIM = 64
MLA_Q_RANK = 256
MLA_KV_RANK = 128
RET_HEADS = 4
RET_QK_DIM = 64
RET_V_DIM = 64
RET_CHUNK = 128
RET_GN_EPS = 1e-6
FFN_HIDDEN = 2816

LANES = 128
HALF = LANES // 2
LOG2E = math.log2(math.e)

_OFF_DQ, _OFF_DK, _OFF_DV = 0, 512, 1024
_OFF_CQ, _OFF_CKV, _OFF_KR = 1536, 1792, 1920
_OFF_RQ, _OFF_RK, _OFF_RV, _OFF_RG = 1952, 2208, 2464, 2720

ATT_BLOCK = 512
ATT_SUB = 256
SUM_ROWS = 16
PRE_BM = ATT_BLOCK
POST_BM = 1024
RET_ROWS = 512
FFN_CHUNK = 256
VMEM_LIMIT = 56 * 1024 * 1024

_NT = (((1,), (1,)), ((), ()))


def _pair_layout(w):
    *lead, n = w.shape
    return w.reshape(*lead, n // LANES, 2, 2, 32).swapaxes(-3, -2).reshape(*lead, n)


def _mla_q_layout(w):
    lead = w.shape[:-1]
    w = w.reshape(*lead, MLA_HEADS, MLA_NOPE_DIM + MLA_ROPE_DIM)
    nope = w[..., :MLA_NOPE_DIM].reshape(*lead, MLA_HEADS, 2, 32)
    rope = w[..., MLA_NOPE_DIM:].reshape(*lead, MLA_HEADS, 2, 16)
    pad = jnp.zeros((*lead, MLA_HEADS, 2, 16), w.dtype)
    return jnp.concatenate([nope, rope, pad], axis=-1).reshape(*lead, MLA_HEADS * LANES)


def _mla_kv_layout(w):
    lead = w.shape[:-1]
    w = w.reshape(*lead, MLA_HEADS, MLA_NOPE_DIM + MLA_V_DIM)
    nope = w[..., :MLA_NOPE_DIM].reshape(*lead, MLA_HEADS, 2, 32)
    pad = jnp.zeros((*lead, MLA_HEADS, 2, 32), w.dtype)
    wk = jnp.concatenate([nope, pad], axis=-1).reshape(*lead, MLA_HEADS * LANES)
    wv = w[..., MLA_NOPE_DIM:].reshape(*lead, MLA_HEADS * MLA_V_DIM)
    return wk, wv


def _mla_krope_layout(w):
    lead = w.shape[:-1]
    kr = w.reshape(*lead, 2, 16)
    return jnp.concatenate([jnp.zeros((*lead, 2, 32), w.dtype), kr,
                            jnp.zeros((*lead, 2, 16), w.dtype)], axis=-1).reshape(*lead, LANES)


def _in_proj_layout(w):
    sl = lambda off, n: w[..., off:off + n]
    return jnp.concatenate([
        _pair_layout(sl(_OFF_DQ, 512)), _pair_layout(sl(_OFF_DK, 512)),
        sl(_OFF_CQ, MLA_Q_RANK), sl(_OFF_CKV, MLA_KV_RANK),
        _mla_krope_layout(sl(_OFF_KR, MLA_ROPE_DIM)),
        _pair_layout(sl(_OFF_RQ, 256)), _pair_layout(sl(_OFF_RK, 256))], axis=-1)


def _in_proj_layout_t(w):
    return jnp.concatenate([w[..., _OFF_DV:_OFF_DV + 512], w[..., _OFF_RV:_OFF_RV + 256],
                            w[..., _OFF_RG:_OFF_RG + 256]], axis=-1).swapaxes(-1, -2)


_P_DQ, _P_DK = 0, 512
_P_CQ, _P_CKV, _P_KR = 1024, 1280, 1408
_P_RQ, _P_RK = 1536, 1792
_P_COLS = 2048
_T_DV, _T_RV, _T_RG = 0, 512, 768
_T_ROWS = 1024


def _rope_lane_freqs():
    f64 = 1.0 / (ROPE_THETA ** (jnp.arange(0, 64, 2, dtype=F32) / 64))
    f32 = 1.0 / (ROPE_THETA ** (jnp.arange(0, 32, 2, dtype=F32) / 32))
    return jnp.concatenate([f64, f32, jnp.zeros((LANES - 48,), F32)])[None, :]


def _rope_table_kernel(pos_ref, invf_ref, c64_ref, s64_ref, cm_ref, sm_ref):
    a = pos_ref[...] * invf_ref[...]
    c, s = jnp.cos(a), jnp.sin(a)
    lane = lax.broadcasted_iota(jnp.int32, (1, LANES), 1)
    sign = jnp.where(lane < HALF, -1.0, 1.0)

    def spread(t, width, period):
        out = t
        for shift in range(period, LANES, period):
            out = jnp.where((lane >= shift) & (lane < shift + width), pltpu.roll(t, shift, 1), out)
        return out

    c64_ref[...] = spread(c, 32, 32)
    s64_ref[...] = spread(s, 32, 32) * sign
    rope = ((lane % HALF) >= 32) & ((lane % HALF) < 48)
    cm_ref[...] = jnp.where(rope, spread(c, 48, HALF), 1.0)
    sm_ref[...] = jnp.where(rope, spread(s, 48, HALF) * sign, 0.0)


def _rope_tables(pos, invf):
    m = pos.shape[0]
    bm = 2048
    tab = jax.ShapeDtypeStruct((m, LANES), F32)
    row = pl.BlockSpec((bm, LANES), lambda i: (i, 0))
    return pl.pallas_call(
        _rope_table_kernel,
        grid=(m // bm,),
        in_specs=[pl.BlockSpec((bm, 1), lambda i: (i, 0)),
                  pl.BlockSpec((1, LANES), lambda i: (0, 0))],
        out_specs=[row, row, row, row],
        out_shape=[tab, tab, tab, tab],
        name="rope_tables",
    )(pos, invf)


def _rms(t, gain, eps):
    return t * lax.rsqrt(jnp.mean(t * t, axis=-1, keepdims=True) + eps) * gain


def _rope_blocks(y, cos, sin):
    outs = []
    for j in range(y.shape[1] // LANES):
        blk = y[:, j * LANES:(j + 1) * LANES]
        outs.append(blk * cos + pltpu.roll(blk, HALF, 1) * sin)
    return outs[0] if len(outs) == 1 else jnp.concatenate(outs, axis=1)


def _pre_kernel(x_ref, g_ref, w_ref, wt_ref, c64_ref, s64_ref, cm_ref, sm_ref,
                qn_ref, wuq_ref, kvn_ref, wuk_ref, wuvt_ref,
                dqt_ref, dk_ref, dvt_ref, mqt_ref, mk_ref, mvt_ref,
                rq_ref, rk_ref, rvt_ref, rgt_ref, *, diff_qscale, mla_qscale, ret_kscale):
    hb = _rms(x_ref[...], g_ref[...], NORM_EPS).astype(BF16)

    def proj(off, n):
        return jnp.dot(hb, w_ref[:, off:off + n], preferred_element_type=F32)

    c64, s64 = c64_ref[...], s64_ref[...]
    cm, sm = cm_ref[...], sm_ref[...]

    dqt_ref[...] = (_rope_blocks(proj(_P_DQ, 512), c64, s64) * diff_qscale).T.astype(BF16)
    dk_ref[...] = _rope_blocks(proj(_P_DK, 512), c64, s64).astype(BF16)
    yt = lax.dot_general(wt_ref[...], hb, _NT, preferred_element_type=F32)
    dvt_ref[...] = yt[_T_DV:_T_DV + 512].astype(BF16)
    rvt_ref[...] = yt[_T_RV:_T_RV + 256].astype(BF16)
    rgt_ref[...] = yt[_T_RG:_T_RG + 256]

    cq = _rms(proj(_P_CQ, MLA_Q_RANK), qn_ref[...], NORM_EPS).astype(BF16)
    q = jnp.dot(cq, wuq_ref[...], preferred_element_type=F32)
    mqt_ref[...] = (_rope_blocks(q, cm, sm) * mla_qscale).T.astype(BF16)
    ckv = _rms(proj(_P_CKV, MLA_KV_RANK), kvn_ref[...], NORM_EPS).astype(BF16)
    kn = jnp.dot(ckv, wuk_ref[...], preferred_element_type=F32)
    kr = _rope_blocks(proj(_P_KR, LANES), cm, sm)
    mk_ref[...] = (kn + jnp.concatenate([kr] * MLA_HEADS, axis=1)).astype(BF16)
    mvt_ref[...] = lax.dot_general(wuvt_ref[...], ckv, _NT,
                                   preferred_element_type=F32).astype(BF16)

    rq_ref[...] = _rope_blocks(proj(_P_RQ, 256), c64, s64).astype(BF16)
    rk_ref[...] = (_rope_blocks(proj(_P_RK, 256), c64, s64) * ret_kscale).astype(BF16)


def _layer_spec(shape, layer):
    return pl.BlockSpec((None, *shape), lambda *_: (layer,) + (0,) * len(shape),
                        pipeline_mode=pl.Buffered(1))


def _pre_attention(layer, x, gain, w_all, wt_all, tabs, qn, wuq, kvn, wuk, wuvt):
    _const_spec = functools.partial(_layer_spec, layer=layer)
    m = x.shape[0]
    bm = PRE_BM
    nb = m // bm
    row = lambda n: pl.BlockSpec((bm, n), lambda i: (i, 0))
    tr = lambda n: pl.BlockSpec((None, n, bm), lambda i: (i, 0, 0))
    out = lambda n, dt=BF16: jax.ShapeDtypeStruct((m, n), dt)
    out_t = lambda n, dt=BF16: jax.ShapeDtypeStruct((nb, n, bm), dt)
    kern = functools.partial(
        _pre_kernel,
        diff_qscale=DIFF_HEAD_DIM ** -0.5 * LOG2E,
        mla_qscale=(MLA_NOPE_DIM + MLA_ROPE_DIM) ** -0.5 * LOG2E,
        ret_kscale=RET_QK_DIM ** -0.5)
    return pl.pallas_call(
        kern,
        grid=(nb,),
        in_specs=[row(D_MODEL), _const_spec((1, D_MODEL)), _const_spec((D_MODEL, _P_COLS)),
                  _const_spec((_T_ROWS, D_MODEL)),
                  row(LANES), row(LANES), row(LANES), row(LANES),
                  _const_spec((1, MLA_Q_RANK)), _const_spec((MLA_Q_RANK, 512)),
                  _const_spec((1, MLA_KV_RANK)), _const_spec((MLA_KV_RANK, 512)),
                  _const_spec((256, MLA_KV_RANK))],
        out_specs=[tr(512), row(512), tr(512), tr(512), row(512), tr(256),
                   row(256), row(256), tr(256), tr(256)],
        out_shape=[out_t(512), out(512), out_t(512), out_t(512), out(512), out_t(256),
                   out(256), out(256), out_t(256), out_t(256, F32)],
        compiler_params=pltpu.CompilerParams(
            dimension_semantics=("parallel",), vmem_limit_bytes=VMEM_LIMIT),
        name="pre_attention",
    )(x, gain, w_all, wt_all, *tabs, qn, wuq, kvn, wuk, wuvt)


def _flash_kernel(*refs, diff, blk, sub, nblk, lam_init):
    if diff:
        qt_ref, k_ref, vt_ref, lam_ref, subln_ref, o_ref, s_sc, mx_sc, m_sc, acc_sc = refs
    else:
        qt_ref, k_ref, vt_ref, o_ref, s_sc, mx_sc, m_sc, acc_sc = refs
    nsub = blk // sub
    entry = 2

    if diff:
        feat = lax.broadcasted_iota(jnp.int32, (LANES, 1), 0)
        map_rows = [((feat % HALF) // 32) == t for t in range(2)]
        q_rows = [slice(0, LANES)] * 2
        k_lanes = [slice(0, LANES)] * 2
        v_rows = [slice(0, LANES)] * 2
    else:
        q_rows = [slice(0, LANES), slice(LANES, 2 * LANES)]
        k_lanes = [slice(0, LANES), slice(LANES, 2 * LANES)]
        v_rows = [slice(0, MLA_V_DIM), slice(MLA_V_DIM, 2 * MLA_V_DIM)]

    def scores(i, j, slot, diagonal=False):
        start = pl.multiple_of(j * blk, blk)
        for t in range(2):
            for c in range(nsub):
                cols = slice(c * sub, (c + 1) * sub)
                nk = (c + 1) * sub if diagonal else blk
                kj = k_ref[pl.ds(start, nk), k_lanes[t]]
                qt = qt_ref[i, q_rows[t], cols]
                if diff:
                    qt = jnp.where(map_rows[t], qt, jnp.zeros_like(qt))
                s = jnp.dot(kj, qt, preferred_element_type=F32)
                s_sc[slot, t, c, 0:nk, :] = s
                if not diagonal:
                    mx_sc[slot, t, :, cols] = jnp.max(s, axis=0, keepdims=True)

    def accumulate(j, slot, diagonal):
        for t in range(2):
            for c in range(nsub):
                cols = slice(c * sub, (c + 1) * sub)
                nk = (c + 1) * sub if diagonal else blk
                s = s_sc[slot, t, c, 0:nk, :]
                if diagonal:
                    key = lax.broadcasted_iota(jnp.int32, (nk, sub), 0)
                    qry = lax.broadcasted_iota(jnp.int32, (nk, sub), 1) + c * sub
                    s = jnp.where(key <= qry, s, -jnp.inf)
                    m_cur = jnp.max(s, axis=0, keepdims=True)
                else:
                    m_cur = mx_sc[slot, t, :, cols]
                m_prev = m_sc[t, :, cols]
                m_new = jnp.maximum(m_prev, m_cur)
                alpha = jnp.exp2(m_prev - m_new)
                p = jnp.exp2(s - m_new)
                vt = jnp.concatenate([vt_ref[j, v_rows[t], 0:nk],
                                      jnp.ones((SUM_ROWS, nk), BF16)], axis=0)
                acc_sc[t, c] = alpha * acc_sc[t, c] + jnp.dot(
                    vt, p.astype(BF16), preferred_element_type=F32)
                m_sc[t, :, cols] = m_new

    if diff:
        lv = lam_ref[...]
        lam = (jnp.exp(jnp.sum(lv[0:1] * lv[1:2], axis=1, keepdims=True))
               - jnp.exp(jnp.sum(lv[2:3] * lv[3:4], axis=1, keepdims=True)) + lam_init)

    def finalize(i):
        for c in range(nsub):
            cols = slice(c * sub, (c + 1) * sub)
            dv = acc_sc.shape[2] - SUM_ROWS
            o0 = acc_sc[0, c, 0:dv, :] * (1.0 / acc_sc[0, c, dv:dv + 1, :])
            o1 = acc_sc[1, c, 0:dv, :] * (1.0 / acc_sc[1, c, dv:dv + 1, :])
            if diff:
                o = o0 - lam * o1
                ms = jnp.mean(o * o, axis=0, keepdims=True)
                o = o * lax.rsqrt(ms + DIFF_SUBLN_EPS) * subln_ref[:, cols] * (1.0 - lam_init)
            else:
                o = jnp.concatenate([o0, o1], axis=0)
            row = pl.multiple_of(i * blk + c * sub, sub)
            o_ref[pl.ds(row, sub), :] = o.T.astype(o_ref.dtype)

    scores(0, 0, entry, diagonal=True)

    def query_block(i, carry):
        i_next = jnp.minimum(i + 1, nblk - 1)
        m_sc[...] = jnp.full(m_sc.shape, -jnp.inf, F32)
        acc_sc[...] = jnp.zeros(acc_sc.shape, F32)

        @pl.when(i == 0)
        def _():
            accumulate(i, entry, True)
            scores(i_next, i_next, entry, diagonal=True)

        @pl.when(i > 0)
        def _():
            scores(i, 0, 0)
            accumulate(i, entry, True)

        def pair(jj, c2):
            j0 = 2 * jj
            scores(i, j0 + 1, 1)
            accumulate(j0, 0, False)
            scores(i, j0 + 2, 0)
            accumulate(j0 + 1, 1, False)
            return c2

        npairs = jnp.maximum(i - 1, 0) // 2

        def quad(qq, c2):
            pair(2 * qq, c2)
            return pair(2 * qq + 1, c2)

        lax.fori_loop(0, npairs // 2, quad, 0)

        @pl.when(npairs % 2 == 1)
        def _():
            pair(npairs - 1, 0)

        @pl.when(i % 2 == 1)
        def _():
            scores(i_next, i_next, entry, diagonal=True)
            accumulate(i - 1, 0, False)

        @pl.when(jnp.logical_and(i % 2 == 0, i > 0))
        def _():
            scores(i, i - 1, 1)
            accumulate(i - 2, 0, False)
            scores(i_next, i_next, entry, diagonal=True)
            accumulate(i - 1, 1, False)

        finalize(i)
        return carry

    lax.fori_loop(0, nblk, query_block, 0)


def _flash_attention(qt, k, vt, diff, layer=0, lam=None, subln=None, lam_init=0.0):
    b, s, _ = k.shape
    blk, sub = ATT_BLOCK, ATT_SUB
    nblk, nsub = s // blk, blk // sub
    width = LANES if diff else 2 * LANES
    groups = k.shape[2] // width
    dv = LANES if diff else MLA_V_DIM
    qspec = pl.BlockSpec((None, nblk, width, blk), lambda bi, g: (bi, 0, g, 0))
    kspec = pl.BlockSpec((None, s, width), lambda bi, g: (bi, 0, g))
    vspec = pl.BlockSpec((None, nblk, LANES, blk), lambda bi, g: (bi, 0, g, 0))
    ospec = pl.BlockSpec((None, s, LANES), lambda bi, g: (bi, 0, g))
    in_specs = [qspec, kspec, vspec]
    args = [qt, k, vt]
    if diff:
        in_specs += [_layer_spec((4, DIFF_HEAD_DIM), layer), _layer_spec((LANES, blk), layer)]
        args += [lam, subln]
    return pl.pallas_call(
        functools.partial(_flash_kernel, diff=diff, blk=blk, sub=sub, nblk=nblk,
                          lam_init=lam_init),
        grid=(b, groups),
        in_specs=in_specs,
        out_specs=ospec,
        out_shape=jax.ShapeDtypeStruct((b, s, groups * LANES), BF16),
        scratch_shapes=[pltpu.VMEM((3, 2, nsub, blk, sub), F32),
                        pltpu.VMEM((3, 2, 1, blk), F32),
                        pltpu.VMEM((2, 1, blk), F32),
                        pltpu.VMEM((2, nsub, dv + SUM_ROWS, sub), F32)],
        compiler_params=pltpu.CompilerParams(
            dimension_semantics=("parallel", "parallel"),
            vmem_limit_bytes=VMEM_LIMIT),
        name="diff_attention" if diff else "mla_attention",
    )(*args)


def _retention_consts():
    h, c = RET_HEADS, RET_CHUNK
    log_gamma = jnp.log1p(-jnp.exp2(-5.0 - jnp.arange(h, dtype=F32)))
    pos = jnp.arange(c, dtype=F32)
    rel = pos[:, None] - pos[None, :]
    decay = jnp.where(rel[None] >= 0,
                      jnp.exp(jnp.maximum(rel, 0.0)[None] * log_gamma[:, None, None]), 0.0)
    xi = jnp.exp((pos[None, :] + 1.0) * log_gamma[:, None])
    zeta = jnp.exp((c - 1.0 - pos)[None, :] * log_gamma[:, None])
    cdec = jnp.exp(c * log_gamma)
    decay_t = jnp.swapaxes(decay, 1, 2).astype(F32)
    xi_row = xi[:, None, :].astype(F32)
    zeta_col = jnp.broadcast_to(zeta[:, :, None], (h, c, LANES)).astype(F32)
    cdec_b = jnp.broadcast_to(cdec[:, None, None], (h, 1, LANES)).astype(F32)
    return decay_t, xi_row, zeta_col, cdec_b


def _retention_kernel(q_ref, k_ref, vt_ref, gt_ref, dec_ref, xi_ref, zeta_ref, cdec_ref,
                      gw_ref, gb_ref, o_ref, st_sc, *, rows):
    @pl.when(pl.program_id(1) == 0)
    def _():
        st_sc[...] = jnp.zeros(st_sc.shape, F32)

    lane = lax.broadcasted_iota(jnp.int32, (1, LANES), 1)
    pair_sel = [((lane % HALF) // 32) == hh for hh in range(2)]
    c = RET_CHUNK
    dv = RET_V_DIM
    for ci in range(rows // c):
        tok = slice(ci * c, (ci + 1) * c)
        for pv in range(RET_HEADS // 2):
            lanes = slice(pv * LANES, (pv + 1) * LANES)
            qp = q_ref[tok, lanes]
            kp = k_ref[tok, lanes]
            vtp = vt_ref[lanes, tok]
            halves = []
            for hh in range(2):
                h = 2 * pv + hh
                qh = jnp.where(pair_sel[hh], qp, jnp.zeros_like(qp))
                kh = jnp.where(pair_sel[hh], kp, jnp.zeros_like(kp))
                st = lax.dot_general(kh, qh, _NT, preferred_element_type=F32)
                inner = jnp.dot(vtp, (st * dec_ref[h]).astype(BF16),
                                preferred_element_type=F32)
                state = st_sc[h]
                cross = lax.dot_general(state.astype(BF16), qh, _NT,
                                        preferred_element_type=F32) * xi_ref[h]
                kz = (kh.astype(F32) * zeta_ref[h]).astype(BF16)
                st_sc[h] = (jnp.dot(vtp, kz, preferred_element_type=F32)
                            + cdec_ref[h] * state)
                o = (inner + cross)[hh * dv:(hh + 1) * dv]
                mu = jnp.mean(o, axis=0, keepdims=True)
                d = o - mu
                var = jnp.mean(d * d, axis=0, keepdims=True)
                halves.append(d * lax.rsqrt(var + RET_GN_EPS))
            on = jnp.concatenate(halves, axis=0) * gw_ref[lanes, :] + gb_ref[lanes, :]
            g = gt_ref[lanes, tok]
            o_ref[tok, lanes] = (g * jax.nn.sigmoid(g) * on).T.astype(o_ref.dtype)


def _retention(layer, q, k, vt, gt, consts, gw, gb):
    b, s, w = q.shape
    rows = RET_ROWS
    c = RET_CHUNK
    blk = pl.BlockSpec((None, rows, w), lambda bi, i: (bi, i, 0))
    blk_t = pl.BlockSpec((None, None, w, rows), lambda bi, i: (bi, i, 0, 0))
    whole = lambda *shape: pl.BlockSpec(shape, lambda bi, i: (0,) * len(shape))
    return pl.pallas_call(
        functools.partial(_retention_kernel, rows=rows),
        grid=(b, s // rows),
        in_specs=[blk, blk, blk_t, blk_t,
                  whole(RET_HEADS, c, c), whole(RET_HEADS, 1, c),
                  whole(RET_HEADS, c, LANES), whole(RET_HEADS, 1, LANES),
                  _layer_spec((w, c), layer), _layer_spec((w, c), layer)],
        out_specs=blk,
        out_shape=jax.ShapeDtypeStruct((b, s, w), BF16),
        scratch_shapes=[pltpu.VMEM((RET_HEADS, LANES, LANES), F32)],
        compiler_params=pltpu.CompilerParams(
            dimension_semantics=("parallel", "arbitrary")),
        name="retention",
    )(q, k, vt, gt, *consts, gw, gb)


def _post_kernel(x_ref, od_ref, om_ref, or_ref, wo_ref, g_ref,
                 wg_ref, wu_ref, wd_ref, fg_ref, o_ref, a_sc, *, final):
    x1 = (x_ref[...]
          + jnp.dot(od_ref[...], wo_ref[0:512, :], preferred_element_type=F32)
          + jnp.dot(om_ref[...], wo_ref[512:768, :], preferred_element_type=F32)
          + jnp.dot(or_ref[...], wo_ref[768:1024, :], preferred_element_type=F32))
    hb = _rms(x1, g_ref[...], NORM_EPS).astype(BF16)
    for c0 in range(0, FFN_HIDDEN, FFN_CHUNK):
        gate = jnp.dot(hb, wg_ref[:, c0:c0 + FFN_CHUNK], preferred_element_type=F32)
        up = jnp.dot(hb, wu_ref[:, c0:c0 + FFN_CHUNK], preferred_element_type=F32)
        a_sc[:, c0:c0 + FFN_CHUNK] = (gate * jax.nn.sigmoid(gate) * up).astype(BF16)
    y = x1 + jnp.dot(a_sc[...], wd_ref[...], preferred_element_type=F32)
    if final:
        y = _rms(y, fg_ref[...], NORM_EPS)
    o_ref[...] = y


def _post_attention(layer, x, od, om, orr, wo, gain, wg, wu, wd, fgain, final):
    _const_spec = functools.partial(_layer_spec, layer=layer)
    m = x.shape[0]
    bm = POST_BM
    row = lambda n: pl.BlockSpec((bm, n), lambda i: (i, 0))
    return pl.pallas_call(
        functools.partial(_post_kernel, final=final),
        grid=(m // bm,),
        in_specs=[row(D_MODEL), row(512), row(256), row(256),
                  _const_spec((D_MODEL, D_MODEL)), _const_spec((1, D_MODEL)),
                  _const_spec((D_MODEL, FFN_HIDDEN)), _const_spec((D_MODEL, FFN_HIDDEN)),
                  _const_spec((FFN_HIDDEN, D_MODEL)),
                  pl.BlockSpec((1, D_MODEL), lambda i: (0, 0))],
        out_specs=row(D_MODEL),
        out_shape=jax.ShapeDtypeStruct((m, D_MODEL), F32),
        scratch_shapes=[pltpu.VMEM((bm, FFN_HIDDEN), BF16)],
        compiler_params=pltpu.CompilerParams(
            dimension_semantics=("parallel",), vmem_limit_bytes=VMEM_LIMIT),
        name="post_attention",
    )(x, od, om, orr, wo, gain, wg, wu, wd, fgain)


def kernel(x, positions, attn_norm, w_in, diff_lam_q1, diff_lam_k1, diff_lam_q2, diff_lam_k2, diff_subln, mla_q_norm, mla_w_uq, mla_kv_norm, mla_w_ukv, ret_gn_w, ret_gn_b, w_out, ffn_norm, w_gate, w_up, w_down, final_norm):
    b, s, d = x.shape
    m = b * s
    nblk = s // ATT_BLOCK
    tabs = _rope_tables(positions.astype(F32).reshape(m, 1), _rope_lane_freqs())
    ret_consts = _retention_consts()

    row_vec = lambda t: t[:, None, :]
    w_all = _in_proj_layout(w_in).astype(BF16)
    wt_all = _in_proj_layout_t(w_in).astype(BF16)
    wuq = _mla_q_layout(mla_w_uq).astype(BF16)
    wuk, wuv = _mla_kv_layout(mla_w_ukv)
    wuk, wuvt = wuk.astype(BF16), wuv.swapaxes(-1, -2).astype(BF16)
    lam = jnp.stack([diff_lam_q1, diff_lam_k1, diff_lam_q2, diff_lam_k2], axis=1)
    subln = jnp.broadcast_to(diff_subln[:, :, None], (DEPTH, LANES, ATT_BLOCK))
    gn_col = lambda t: jnp.broadcast_to(t[:, :, None], (*t.shape, RET_CHUNK))
    gn_w, gn_b = gn_col(ret_gn_w), gn_col(ret_gn_b)
    wo, wg, wu, wd = (t.astype(BF16) for t in (w_out, w_gate, w_up, w_down))
    attn_g, ffn_g, qn_g, kvn_g = (row_vec(t) for t in (attn_norm, ffn_norm, mla_q_norm, mla_kv_norm))

    sh = lambda t: t.reshape(b, s, t.shape[-1])
    sh_t = lambda t: t.reshape(b, nblk, t.shape[1], ATT_BLOCK)
    xf = x.reshape(m, d)
    for layer in range(DEPTH):
        lam_init = 0.8 - 0.6 * math.exp(-0.3 * layer)
        (dqt, dk, dvt, mqt, mk, mvt, rq, rk, rvt, rgt) = _pre_attention(
            layer, xf, attn_g, w_all, wt_all, tabs, qn_g, wuq, kvn_g, wuk, wuvt)
        o_diff = _flash_attention(sh_t(dqt), sh(dk), sh_t(dvt), True, layer=layer, lam=lam,
                                  subln=subln, lam_init=lam_init)
        o_mla = _flash_attention(sh_t(mqt), sh(mk), sh_t(mvt), False)
        o_ret = _retention(layer, sh(rq), sh(rk), sh_t(rvt), sh_t(rgt), ret_consts, gn_w, gn_b)
        xf = _post_attention(
            layer, xf, o_diff.reshape(m, 512), o_mla.reshape(m, 256), o_ret.reshape(m, 256),
            wo, ffn_g, wg, wu, wd, final_norm[None], final=(layer == DEPTH - 1))
    return xf.reshape(b, s, d)
```

```python
import functools
import math

import jax
import jax.numpy as jnp
from jax import lax
from jax.experimental import pallas as pl
from jax.experimental.pallas import tpu as pltpu

F32 = jnp.float32
BF16 = jnp.bfloat16

D_MODEL = 1024
DEPTH = 2
ROPE_THETA = 10000.0
NORM_EPS = 1e-6

DIFF_HEADS = 4
DIFF_HEAD_DIM = 64
DIFF_SUBLN_EPS = 1e-5
MLA_HEADS = 4
MLA_NOPE_DIM = 64
MLA_ROPE_DIM = 32
MLA_V_DIM = 64
MLA_Q_RANK = 256
MLA_KV_RANK = 128
RET_HEADS = 4
RET_QK_DIM = 64
RET_V_DIM = 64
RET_CHUNK = 128
RET_GN_EPS = 1e-6
FFN_HIDDEN = 2816

LANES = 128
HALF = LANES // 2
LOG2E = math.log2(math.e)

_OFF_DQ, _OFF_DK, _OFF_DV = 0, 512, 1024
_OFF_CQ, _OFF_CKV, _OFF_KR = 1536, 1792, 1920
_OFF_RQ, _OFF_RK, _OFF_RV, _OFF_RG = 1952, 2208, 2464, 2720

ATT_BLOCK = 512
ATT_SUB = 256
SUM_ROWS = 16
TOKEN_BLOCK = ATT_BLOCK
PRE_BM = 1024
POST_BM = 1024
RET_ROWS = 1024
FFN_CHUNK = 256
VMEM_LIMIT = 56 * 1024 * 1024

_NT = (((1,), (1,)), ((), ()))


def _pair_layout(w):
    *lead, n = w.shape
    return w.reshape(*lead, n // LANES, 2, 2, 32).swapaxes(-3, -2).reshape(*lead, n)


def _mla_q_layout(w):
    lead = w.shape[:-1]
    w = w.reshape(*lead, MLA_HEADS, MLA_NOPE_DIM + MLA_ROPE_DIM)
    nope = w[..., :MLA_NOPE_DIM].reshape(*lead, MLA_HEADS, 2, 32)
    rope = w[..., MLA_NOPE_DIM:].reshape(*lead, MLA_HEADS, 2, 16)
    pad = jnp.zeros((*lead, MLA_HEADS, 2, 16), w.dtype)
    return jnp.concatenate([nope, rope, pad], axis=-1).reshape(*lead, MLA_HEADS * LANES)


def _mla_kv_layout(w):
    lead = w.shape[:-1]
    w = w.reshape(*lead, MLA_HEADS, MLA_NOPE_DIM + MLA_V_DIM)
    nope = w[..., :MLA_NOPE_DIM].reshape(*lead, MLA_HEADS, 2, 32)
    pad = jnp.zeros((*lead, MLA_HEADS, 2, 32), w.dtype)
    wk = jnp.concatenate([nope, pad], axis=-1).reshape(*lead, MLA_HEADS * LANES)
    wv = w[..., MLA_NOPE_DIM:].reshape(*lead, MLA_HEADS * MLA_V_DIM)
    return wk, wv


def _mla_krope_layout(w):
    lead = w.shape[:-1]
    kr = w.reshape(*lead, 2, 16)
    return jnp.concatenate([jnp.zeros((*lead, 2, 32), w.dtype), kr,
                            jnp.zeros((*lead, 2, 16), w.dtype)], axis=-1).reshape(*lead, LANES)


def _in_proj_layout(w):
    sl = lambda off, n: w[..., off:off + n]
    return jnp.concatenate([
        _pair_layout(sl(_OFF_DQ, 512)), _pair_layout(sl(_OFF_DK, 512)),
        sl(_OFF_CQ, MLA_Q_RANK), sl(_OFF_CKV, MLA_KV_RANK),
        _mla_krope_layout(sl(_OFF_KR, MLA_ROPE_DIM)),
        _pair_layout(sl(_OFF_RQ, 256)), _pair_layout(sl(_OFF_RK, 256))], axis=-1)


def _in_proj_layout_t(w):
    return jnp.concatenate([w[..., _OFF_DV:_OFF_DV + 512], w[..., _OFF_RV:_OFF_RV + 256],
                            w[..., _OFF_RG:_OFF_RG + 256]], axis=-1).swapaxes(-1, -2)


_P_DQ, _P_DK = 0, 512
_P_CQ, _P_CKV, _P_KR = 1024, 1280, 1408
_P_RQ, _P_RK = 1536, 1792
_P_COLS = 2048
_T_DV, _T_RV, _T_RG = 0, 512, 768
_T_ROWS = 1024


def _rope_lane_freqs():
    f64 = 1.0 / (ROPE_THETA ** (jnp.arange(0, 64, 2, dtype=F32) / 64))
    f32 = 1.0 / (ROPE_THETA ** (jnp.arange(0, 32, 2, dtype=F32) / 32))
    half = jnp.concatenate([f64, f32, jnp.zeros((HALF - 48,), F32)])
    return jnp.concatenate([half, half])[None, :]


def _rope_table_kernel(pos_ref, invf_ref, c64_ref, s64_ref, cm_ref, sm_ref):
    half = pos_ref.shape[0] // 2
    lane = lax.broadcasted_iota(jnp.int32, (1, LANES), 1)
    a = jnp.where(lane < HALF, pos_ref[0:half, :], pos_ref[half:2 * half, :]) * invf_ref[...]
    c2, s2 = jnp.cos(a), jnp.sin(a)
    sign = jnp.where(lane < HALF, -1.0, 1.0)

    def spread(t, width, period):
        out = t
        for shift in range(period, LANES, period):
            out = jnp.where((lane >= shift) & (lane < shift + width), pltpu.roll(t, shift, 1), out)
        return out

    rope = ((lane % HALF) >= 32) & ((lane % HALF) < 48)
    for part in range(2):
        rows = slice(part * half, (part + 1) * half)
        c = c2 if part == 0 else pltpu.roll(c2, HALF, 1)
        s = s2 if part == 0 else pltpu.roll(s2, HALF, 1)
        c64_ref[rows, :] = spread(c, 32, 32)
        s64_ref[rows, :] = spread(s, 32, 32) * sign
        cm_ref[rows, :] = jnp.where(rope, spread(c, 48, HALF), 1.0)
        sm_ref[rows, :] = jnp.where(rope, spread(s, 48, HALF) * sign, 0.0)


def _rope_tables(pos, invf):
    m = pos.shape[0]
    bm = 2048
    tab = jax.ShapeDtypeStruct((m, LANES), F32)
    row = pl.BlockSpec((bm, LANES), lambda i: (i, 0))
    return pl.pallas_call(
        _rope_table_kernel,
        grid=(m // bm,),
        in_specs=[pl.BlockSpec((bm, 1), lambda i: (i, 0)),
                  pl.BlockSpec((1, LANES), lambda i: (0, 0))],
        out_specs=[row, row, row, row],
        out_shape=[tab, tab, tab, tab],
        name="rope_tables",
    )(pos, invf)


def _rms(t, gain, eps):
    return t * lax.rsqrt(jnp.mean(t * t, axis=-1, keepdims=True) + eps) * gain


def _rope_blocks(y, cos, sin):
    outs = []
    for j in range(y.shape[1] // LANES):
        blk = y[:, j * LANES:(j + 1) * LANES]
        outs.append(blk * cos + pltpu.roll(blk, HALF, 1) * sin)
    return outs[0] if len(outs) == 1 else jnp.concatenate(outs, axis=1)


def _pre_kernel(x_ref, g_ref, w_ref, wt_ref, c64_ref, s64_ref, cm_ref, sm_ref,
                qn_ref, wuq_ref, kvn_ref, wuk_ref, wuvt_ref,
                dqt_ref, dk_ref, dvt_ref, mqt_ref, mk_ref, mvt_ref,
                rq_ref, rk_ref, rvt_ref, rgt_ref, *, diff_qscale, mla_qscale, ret_kscale):
    hb = _rms(x_ref[...], g_ref[...], NORM_EPS).astype(BF16)

    def proj(off, n):
        return jnp.dot(hb, w_ref[:, off:off + n], preferred_element_type=F32)

    c64, s64 = c64_ref[...], s64_ref[...]
    cm, sm = cm_ref[...], sm_ref[...]

    def store_t(ref, yt):
        for tb in range(ref.shape[0]):
            ref[tb] = yt[:, tb * TOKEN_BLOCK:(tb + 1) * TOKEN_BLOCK].astype(ref.dtype)

    store_t(dqt_ref, (_rope_blocks(proj(_P_DQ, 512), c64, s64) * diff_qscale).T)
    dk_ref[...] = _rope_blocks(proj(_P_DK, 512), c64, s64).astype(BF16)
    yt = lax.dot_general(wt_ref[...], hb, _NT, preferred_element_type=F32)
    store_t(dvt_ref, yt[_T_DV:_T_DV + 512])
    store_t(rvt_ref, yt[_T_RV:_T_RV + 256])
    store_t(rgt_ref, yt[_T_RG:_T_RG + 256])

    cq = _rms(proj(_P_CQ, MLA_Q_RANK), qn_ref[...], NORM_EPS).astype(BF16)
    q = jnp.dot(cq, wuq_ref[...], preferred_element_type=F32)
    store_t(mqt_ref, (_rope_blocks(q, cm, sm) * mla_qscale).T)
    ckv = _rms(proj(_P_CKV, MLA_KV_RANK), kvn_ref[...], NORM_EPS).astype(BF16)
    kn = jnp.dot(ckv, wuk_ref[...], preferred_element_type=F32)
    kr = _rope_blocks(proj(_P_KR, LANES), cm, sm)
    mk_ref[...] = (kn + jnp.concatenate([kr] * MLA_HEADS, axis=1)).astype(BF16)
    store_t(mvt_ref, lax.dot_general(wuvt_ref[...], ckv, _NT,
                                     preferred_element_type=F32))

    rq_ref[...] = _rope_blocks(proj(_P_RQ, 256), c64, s64).astype(BF16)
    rk_ref[...] = (_rope_blocks(proj(_P_RK, 256), c64, s64) * ret_kscale).astype(BF16)


def _layer_spec(shape, layer):
    return pl.BlockSpec((None, *shape), lambda *_: (layer,) + (0,) * len(shape),
                        pipeline_mode=pl.Buffered(1))


def _pre_attention(layer, x, gain, w_all, wt_all, tabs, qn, wuq, kvn, wuk, wuvt):
    _const_spec = functools.partial(_layer_spec, layer=layer)
    m = x.shape[0]
    bm = PRE_BM
    nb = m // bm
    row = lambda n: pl.BlockSpec((bm, n), lambda i: (i, 0))
    tpb = bm // TOKEN_BLOCK
    tr = lambda n: pl.BlockSpec((tpb, n, TOKEN_BLOCK), lambda i: (i, 0, 0))
    out = lambda n, dt=BF16: jax.ShapeDtypeStruct((m, n), dt)
    out_t = lambda n, dt=BF16: jax.ShapeDtypeStruct((m // TOKEN_BLOCK, n, TOKEN_BLOCK), dt)
    kern = functools.partial(
        _pre_kernel,
        diff_qscale=DIFF_HEAD_DIM ** -0.5 * LOG2E,
        mla_qscale=(MLA_NOPE_DIM + MLA_ROPE_DIM) ** -0.5 * LOG2E,
        ret_kscale=RET_QK_DIM ** -0.5)
    return pl.pallas_call(
        kern,
        grid=(nb,),
        in_specs=[row(D_MODEL), _const_spec((1, D_MODEL)), _const_spec((D_MODEL, _P_COLS)),
                  _const_spec((_T_ROWS, D_MODEL)),
                  row(LANES), row(LANES), row(LANES), row(LANES),
                  _const_spec((1, MLA_Q_RANK)), _const_spec((MLA_Q_RANK, 512)),
                  _const_spec((1, MLA_KV_RANK)), _const_spec((MLA_KV_RANK, 512)),
                  _const_spec((256, MLA_KV_RANK))],
        out_specs=[tr(512), row(512), tr(512), tr(512), row(512), tr(256),
                   row(256), row(256), tr(256), tr(256)],
        out_shape=[out_t(512), out(512), out_t(512), out_t(512), out(512), out_t(256),
                   out(256), out(256), out_t(256), out_t(256, F32)],
        compiler_params=pltpu.CompilerParams(
            dimension_semantics=("parallel",), vmem_limit_bytes=VMEM_LIMIT),
        name="pre_attention",
    )(x, gain, w_all, wt_all, *tabs, qn, wuq, kvn, wuk, wuvt)


def _flash_kernel(*refs, diff, blk, sub, nblk, lam_init):
    if diff:
        qt_ref, k_ref, vt_ref, lam_ref, subln_ref, o_ref, s_sc, mx_sc, m_sc, acc_sc = refs
    else:
        qt_ref, k_ref, vt_ref, o_ref, s_sc, mx_sc, m_sc, acc_sc = refs
    nsub = blk // sub
    entry = 2

    if diff:
        feat = lax.broadcasted_iota(jnp.int32, (LANES, 1), 0)
        map_rows = [((feat % HALF) // 32) == t for t in range(2)]
        q_rows = [slice(0, LANES)] * 2
        k_lanes = [slice(0, LANES)] * 2
        v_rows = [slice(0, LANES)] * 2
    else:
        q_rows = [slice(0, LANES), slice(LANES, 2 * LANES)]
        k_lanes = [slice(0, LANES), slice(LANES, 2 * LANES)]
        v_rows = [slice(0, MLA_V_DIM), slice(MLA_V_DIM, 2 * MLA_V_DIM)]

    def scores(i, j, slot, diagonal=False):
        start = pl.multiple_of(j * blk, blk)
        for t in range(2):
            for c in range(nsub):
                cols = slice(c * sub, (c + 1) * sub)
                nk = (c + 1) * sub if diagonal else blk
                kj = k_ref[pl.ds(start, nk), k_lanes[t]]
                qt = qt_ref[i, q_rows[t], cols]
                if diff:
                    qt = jnp.where(map_rows[t], qt, jnp.zeros_like(qt))
                s = jnp.dot(kj, qt, preferred_element_type=F32)
                s_sc[slot, t, c, 0:nk, :] = s
                if not diagonal:
                    mx_sc[slot, t, :, cols] = jnp.max(s, axis=0, keepdims=True)

    def accumulate(j, slot, diagonal):
        for t in range(2):
            for c in range(nsub):
                cols = slice(c * sub, (c + 1) * sub)
                nk = (c + 1) * sub if diagonal else blk
                s = s_sc[slot, t, c, 0:nk, :]
                if diagonal:
                    key = lax.broadcasted_iota(jnp.int32, (nk, sub), 0)
                    qry = lax.broadcasted_iota(jnp.int32, (nk, sub), 1) + c * sub
                    s = jnp.where(key <= qry, s, -jnp.inf)
                    m_cur = jnp.max(s, axis=0, keepdims=True)
                else:
                    m_cur = mx_sc[slot, t, :, cols]
                m_prev = m_sc[t, :, cols]
                m_new = jnp.maximum(m_prev, m_cur)
                alpha = jnp.exp2(m_prev - m_new)
                p = jnp.exp2(s - m_new)
                vt = jnp.concatenate([vt_ref[j, v_rows[t], 0:nk],
                                      jnp.ones((SUM_ROWS, nk), BF16)], axis=0)
                acc_sc[t, c] = alpha * acc_sc[t, c] + jnp.dot(
                    vt, p.astype(BF16), preferred_element_type=F32)
                m_sc[t, :, cols] = m_new

    if diff:
        lv = lam_ref[...]
        lam = (jnp.exp(jnp.sum(lv[0:1] * lv[1:2], axis=1, keepdims=True))
               - jnp.exp(jnp.sum(lv[2:3] * lv[3:4], axis=1, keepdims=True)) + lam_init)

    def finalize(i):
        for c in range(nsub):
            cols = slice(c * sub, (c + 1) * sub)
            dv = acc_sc.shape[2] - SUM_ROWS
            o0 = acc_sc[0, c, 0:dv, :] * (1.0 / acc_sc[0, c, dv:dv + 1, :])
            o1 = acc_sc[1, c, 0:dv, :] * (1.0 / acc_sc[1, c, dv:dv + 1, :])
            if diff:
                o = o0 - lam * o1
                ms = jnp.mean(o * o, axis=0, keepdims=True)
                o = o * lax.rsqrt(ms + DIFF_SUBLN_EPS) * subln_ref[:, cols] * (1.0 - lam_init)
            else:
                o = jnp.concatenate([o0, o1], axis=0)
            row = pl.multiple_of(i * blk + c * sub, sub)
            o_ref[pl.ds(row, sub), :] = o.T.astype(o_ref.dtype)

    scores(0, 0, entry, diagonal=True)

    def query_block(i, carry):
        i_next = jnp.minimum(i + 1, nblk - 1)
        m_sc[...] = jnp.full(m_sc.shape, -jnp.inf, F32)
        acc_sc[...] = jnp.zeros(acc_sc.shape, F32)

        @pl.when(i == 0)
        def _():
            accumulate(i, entry, True)
            scores(i_next, i_next, entry, diagonal=True)

        @pl.when(i > 0)
        def _():
            scores(i, 0, 0)
            accumulate(i, entry, True)

        def pair(jj, c2):
            j0 = 2 * jj
            scores(i, j0 + 1, 1)
            accumulate(j0, 0, False)
            scores(i, j0 + 2, 0)
            accumulate(j0 + 1, 1, False)
            return c2

        npairs = jnp.maximum(i - 1, 0) // 2

        def quad(qq, c2):
            pair(2 * qq, c2)
            return pair(2 * qq + 1, c2)

        lax.fori_loop(0, npairs // 2, quad, 0)

        @pl.when(npairs % 2 == 1)
        def _():
            pair(npairs - 1, 0)

        @pl.when(i % 2 == 1)
        def _():
            scores(i_next, i_next, entry, diagonal=True)
            accumulate(i - 1, 0, False)

        @pl.when(jnp.logical_and(i % 2 == 0, i > 0))
        def _():
            scores(i, i - 1, 1)
            accumulate(i - 2, 0, False)
            scores(i_next, i_next, entry, diagonal=True)
            accumulate(i - 1, 1, False)

        finalize(i)
        return carry

    lax.fori_loop(0, nblk, query_block, 0)


def _flash_attention(qt, k, vt, diff, layer=0, lam=None, subln=None, lam_init=0.0):
    b, s, _ = k.shape
    blk, sub = ATT_BLOCK, ATT_SUB
    nblk, nsub = s // blk, blk // sub
    width = LANES if diff else 2 * LANES
    groups = k.shape[2] // width
    dv = LANES if diff else MLA_V_DIM
    qspec = pl.BlockSpec((None, nblk, width, blk), lambda bi, g: (bi, 0, g, 0))
    kspec = pl.BlockSpec((None, s, width), lambda bi, g: (bi, 0, g))
    vspec = pl.BlockSpec((None, nblk, LANES, blk), lambda bi, g: (bi, 0, g, 0))
    ospec = pl.BlockSpec((None, s, LANES), lambda bi, g: (bi, 0, g))
    in_specs = [qspec, kspec, vspec]
    args = [qt, k, vt]
    if diff:
        in_specs += [_layer_spec((4, DIFF_HEAD_DIM), layer), _layer_spec((LANES, blk), layer)]
        args += [lam, subln]
    return pl.pallas_call(
        functools.partial(_flash_kernel, diff=diff, blk=blk, sub=sub, nblk=nblk,
                          lam_init=lam_init),
        grid=(b, groups),
        in_specs=in_specs,
        out_specs=ospec,
        out_shape=jax.ShapeDtypeStruct((b, s, groups * LANES), BF16),
        scratch_shapes=[pltpu.VMEM((3, 2, nsub, blk, sub), F32),
                        pltpu.VMEM((3, 2, 1, blk), F32),
                        pltpu.VMEM((2, 1, blk), F32),
                        pltpu.VMEM((2, nsub, dv + SUM_ROWS, sub), F32)],
        compiler_params=pltpu.CompilerParams(
            dimension_semantics=("parallel", "parallel"),
            vmem_limit_bytes=VMEM_LIMIT),
        name="diff_attention" if diff else "mla_attention",
    )(*args)


def _retention_consts():
    h, c = RET_HEADS, RET_CHUNK
    log_gamma = jnp.log1p(-jnp.exp2(-5.0 - jnp.arange(h, dtype=F32)))
    pos = jnp.arange(c, dtype=F32)
    rel = pos[:, None] - pos[None, :]
    decay = jnp.where(rel[None] >= 0,
                      jnp.exp(jnp.maximum(rel, 0.0)[None] * log_gamma[:, None, None]), 0.0)
    xi = jnp.exp((pos[None, :] + 1.0) * log_gamma[:, None])
    zeta = jnp.exp((c - 1.0 - pos)[None, :] * log_gamma[:, None])
    cdec = jnp.exp(c * log_gamma)
    decay_t = jnp.swapaxes(decay, 1, 2).astype(F32)
    xi_row = xi[:, None, :].astype(F32)
    zeta_col = jnp.broadcast_to(zeta[:, :, None], (h, c, LANES)).astype(F32)
    cdec_b = jnp.broadcast_to(cdec[:, None, None], (h, 1, LANES)).astype(F32)
    return decay_t, xi_row, zeta_col, cdec_b


def _retention_kernel(q_ref, k_ref, vt_ref, gt_ref, dec_ref, xi_ref, zeta_ref, cdec_ref,
                      gw_ref, gb_ref, o_ref, st_sc, *, rows):
    @pl.when(pl.program_id(1) == 0)
    def _():
        st_sc[...] = jnp.zeros(st_sc.shape, F32)

    lane = lax.broadcasted_iota(jnp.int32, (1, LANES), 1)
    pair_sel = [((lane % HALF) // 32) == hh for hh in range(2)]
    c = RET_CHUNK
    dv = RET_V_DIM
    for ci in range(rows // c):
        tok = slice(ci * c, (ci + 1) * c)
        tb = (ci * c) // TOKEN_BLOCK
        tok_t = slice(ci * c - tb * TOKEN_BLOCK, (ci + 1) * c - tb * TOKEN_BLOCK)
        for pv in range(RET_HEADS // 2):
            lanes = slice(pv * LANES, (pv + 1) * LANES)
            qp = q_ref[tok, lanes]
            kp = k_ref[tok, lanes]
            vtp = vt_ref[tb, lanes, tok_t]
            halves = []
            for hh in range(2):
                h = 2 * pv + hh
                qh = jnp.where(pair_sel[hh], qp, jnp.zeros_like(qp))
                kh = jnp.where(pair_sel[hh], kp, jnp.zeros_like(kp))
                st = lax.dot_general(kh, qh, _NT, preferred_element_type=F32)
                inner = jnp.dot(vtp, (st * dec_ref[h]).astype(BF16),
                                preferred_element_type=F32)
                state = st_sc[h]
                cross = lax.dot_general(state.astype(BF16), qh, _NT,
                                        preferred_element_type=F32) * xi_ref[h]
                kz = (kh.astype(F32) * zeta_ref[h]).astype(BF16)
                st_sc[h] = (jnp.dot(vtp, kz, preferred_element_type=F32)
                            + cdec_ref[h] * state)
                o = (inner + cross)[hh * dv:(hh + 1) * dv]
                mu = jnp.mean(o, axis=0, keepdims=True)
                d = o - mu
                var = jnp.mean(d * d, axis=0, keepdims=True)
                halves.append(d * lax.rsqrt(var + RET_GN_EPS))
            on = jnp.concatenate(halves, axis=0) * gw_ref[lanes, :] + gb_ref[lanes, :]
            g = gt_ref[tb, lanes, tok_t]
            o_ref[tok, lanes] = (g * jax.nn.sigmoid(g) * on).T.astype(o_ref.dtype)


def _retention(layer, q, k, vt, gt, consts, gw, gb):
    b, s, w = q.shape
    rows = RET_ROWS
    c = RET_CHUNK
    blk = pl.BlockSpec((None, rows, w), lambda bi, i: (bi, i, 0))
    blk_t = pl.BlockSpec((None, rows // TOKEN_BLOCK, w, TOKEN_BLOCK), lambda bi, i: (bi, i, 0, 0))
    whole = lambda *shape: pl.BlockSpec(shape, lambda bi, i: (0,) * len(shape))
    return pl.pallas_call(
        functools.partial(_retention_kernel, rows=rows),
        grid=(b, s // rows),
        in_specs=[blk, blk, blk_t, blk_t,
                  whole(RET_HEADS, c, c), whole(RET_HEADS, 1, c),
                  whole(RET_HEADS, c, LANES), whole(RET_HEADS, 1, LANES),
                  _layer_spec((w, c), layer), _layer_spec((w, c), layer)],
        out_specs=blk,
        out_shape=jax.ShapeDtypeStruct((b, s, w), BF16),
        scratch_shapes=[pltpu.VMEM((RET_HEADS, LANES, LANES), F32)],
        compiler_params=pltpu.CompilerParams(
            dimension_semantics=("parallel", "arbitrary")),
        name="retention",
    )(q, k, vt, gt, *consts, gw, gb)


def _post_kernel(x_ref, od_ref, om_ref, or_ref, wo_ref, g_ref,
                 wg_ref, wu_ref, wd_ref, fg_ref, o_ref, a_sc, *, final):
    x1 = (x_ref[...]
          + jnp.dot(od_ref[...], wo_ref[0:512, :], preferred_element_type=F32)
          + jnp.dot(om_ref[...], wo_ref[512:768, :], preferred_element_type=F32)
          + jnp.dot(or_ref[...], wo_ref[768:1024, :], preferred_element_type=F32))
    hb = _rms(x1, g_ref[...], NORM_EPS).astype(BF16)
    for c0 in range(0, FFN_HIDDEN, FFN_CHUNK):
        gate = jnp.dot(hb, wg_ref[:, c0:c0 + FFN_CHUNK], preferred_element_type=F32)
        up = jnp.dot(hb, wu_ref[:, c0:c0 + FFN_CHUNK], preferred_element_type=F32)
        a_sc[:, c0:c0 + FFN_CHUNK] = (gate * jax.nn.sigmoid(gate) * up).astype(BF16)
    y = x1 + jnp.dot(a_sc[...], wd_ref[...], preferred_element_type=F32)
    if final:
        y = _rms(y, fg_ref[...], NORM_EPS)
    o_ref[...] = y


def _post_attention(layer, x, od, om, orr, wo, gain, wg, wu, wd, fgain, final):
    _const_spec = functools.partial(_layer_spec, layer=layer)
    m = x.shape[0]
    bm = POST_BM
    row = lambda n: pl.BlockSpec((bm, n), lambda i: (i, 0))
    return pl.pallas_call(
        functools.partial(_post_kernel, final=final),
        grid=(m // bm,),
        in_specs=[row(D_MODEL), row(512), row(256), row(256),
                  _const_spec((D_MODEL, D_MODEL)), _const_spec((1, D_MODEL)),
                  _const_spec((D_MODEL, FFN_HIDDEN)), _const_spec((D_MODEL, FFN_HIDDEN)),
                  _const_spec((FFN_HIDDEN, D_MODEL)),
                  pl.BlockSpec((1, D_MODEL), lambda i: (0, 0))],
        out_specs=row(D_MODEL),
        out_shape=jax.ShapeDtypeStruct((m, D_MODEL), F32),
        scratch_shapes=[pltpu.VMEM((bm, FFN_HIDDEN), BF16)],
        compiler_params=pltpu.CompilerParams(
            dimension_semantics=("parallel",), vmem_limit_bytes=VMEM_LIMIT),
        name="post_attention",
    )(x, od, om, orr, wo, gain, wg, wu, wd, fgain)


def kernel(x, positions, attn_norm, w_in, diff_lam_q1, diff_lam_k1, diff_lam_q2, diff_lam_k2, diff_subln, mla_q_norm, mla_w_uq, mla_kv_norm, mla_w_ukv, ret_gn_w, ret_gn_b, w_out, ffn_norm, w_gate, w_up, w_down, final_norm):
    b, s, d = x.shape
    m = b * s
    nblk = s // ATT_BLOCK
    tabs = _rope_tables(positions.astype(F32).reshape(m, 1), _rope_lane_freqs())
    ret_consts = _retention_consts()

    row_vec = lambda t: t[:, None, :]
    w_all = _in_proj_layout(w_in).astype(BF16)
    wt_all = _in_proj_layout_t(w_in).astype(BF16)
    wuq = _mla_q_layout(mla_w_uq).astype(BF16)
    wuk, wuv = _mla_kv_layout(mla_w_ukv)
    wuk, wuvt = wuk.astype(BF16), wuv.swapaxes(-1, -2).astype(BF16)
    lam = jnp.stack([diff_lam_q1, diff_lam_k1, diff_lam_q2, diff_lam_k2], axis=1)
    subln = jnp.broadcast_to(diff_subln[:, :, None], (DEPTH, LANES, ATT_BLOCK))
    gn_col = lambda t: jnp.broadcast_to(t[:, :, None], (*t.shape, RET_CHUNK))
    gn_w, gn_b = gn_col(ret_gn_w), gn_col(ret_gn_b)
    wo, wg, wu, wd = (t.astype(BF16) for t in (w_out, w_gate, w_up, w_down))
    attn_g, ffn_g, qn_g, kvn_g = (row_vec(t) for t in (attn_norm, ffn_norm, mla_q_norm, mla_kv_norm))

    sh = lambda t: t.reshape(b, s, t.shape[-1])
    sh_t = lambda t: t.reshape(b, nblk, t.shape[1], ATT_BLOCK)
    xf = x.reshape(m, d)
    for layer in range(DEPTH):
        lam_init = 0.8 - 0.6 * math.exp(-0.3 * layer)
        (dqt, dk, dvt, mqt, mk, mvt, rq, rk, rvt, rgt) = _pre_attention(
            layer, xf, attn_g, w_all, wt_all, tabs, qn_g, wuq, kvn_g, wuk, wuvt)
        o_diff = _flash_attention(sh_t(dqt), sh(dk), sh_t(dvt), True, layer=layer, lam=lam,
                                  subln=subln, lam_init=lam_init)
        o_mla = _flash_attention(sh_t(mqt), sh(mk), sh_t(mvt), False)
        o_ret = _retention(layer, sh(rq), sh(rk), sh_t(rvt), sh_t(rgt), ret_consts, gn_w, gn_b)
        xf = _post_attention(
            layer, xf, o_diff.reshape(m, 512), o_mla.reshape(m, 256), o_ret.reshape(m, 256),
            wo, ffn_g, wg, wu, wd, final_norm[None], final=(layer == DEPTH - 1))
    return xf.reshape(b, s, d)
```

```python
import functools
import math

import numpy as np
import jax
import jax.numpy as jnp
from jax import lax
from jax.experimental import pallas as pl
from jax.experimental.pallas import tpu as pltpu

F32 = jnp.float32
BF16 = jnp.bfloat16

D_MODEL = 1024
DEPTH = 2
ROPE_THETA = 10000.0
NORM_EPS = 1e-6

DIFF_HEADS = 4
DIFF_HEAD_DIM = 64
DIFF_SUBLN_EPS = 1e-5
MLA_HEADS = 4
MLA_NOPE_DIM = 64
MLA_ROPE_DIM = 32
MLA_V_DIM = 64
MLA_Q_RANK = 256
MLA_KV_RANK = 128
RET_HEADS = 4
RET_QK_DIM = 64
RET_V_DIM = 64
RET_CHUNK = 128
RET_GN_EPS = 1e-6
FFN_HIDDEN = 2816

LANES = 128
HALF = LANES // 2
LOG2E = math.log2(math.e)

_OFF_DQ, _OFF_DK, _OFF_DV = 0, 512, 1024
_OFF_CQ, _OFF_CKV, _OFF_KR = 1536, 1792, 1920
_OFF_RQ, _OFF_RK, _OFF_RV, _OFF_RG = 1952, 2208, 2464, 2720

ATT_BLOCK = 512
ATT_SUB = 256
SUM_ROWS = 16
TOKEN_BLOCK = ATT_BLOCK
PRE_BM = 1024
POST_BM = 1024
RET_ROWS = 1024
FFN_CHUNK = 256
VMEM_LIMIT = 56 * 1024 * 1024

_NT = (((1,), (1,)), ((), ()))


def _pair_layout(w):
    *lead, n = w.shape
    return w.reshape(*lead, n // LANES, 2, 2, 32).swapaxes(-3, -2).reshape(*lead, n)


def _mla_q_layout(w):
    lead = w.shape[:-1]
    w = w.reshape(*lead, MLA_HEADS, MLA_NOPE_DIM + MLA_ROPE_DIM)
    nope = w[..., :MLA_NOPE_DIM].reshape(*lead, MLA_HEADS, 2, 32)
    rope = w[..., MLA_NOPE_DIM:].reshape(*lead, MLA_HEADS, 2, 16)
    pad = jnp.zeros((*lead, MLA_HEADS, 2, 16), w.dtype)
    return jnp.concatenate([nope, rope, pad], axis=-1).reshape(*lead, MLA_HEADS * LANES)


def _mla_kv_layout(w):
    lead = w.shape[:-1]
    w = w.reshape(*lead, MLA_HEADS, MLA_NOPE_DIM + MLA_V_DIM)
    nope = w[..., :MLA_NOPE_DIM].reshape(*lead, MLA_HEADS, 2, 32)
    pad = jnp.zeros((*lead, MLA_HEADS, 2, 32), w.dtype)
    wk = jnp.concatenate([nope, pad], axis=-1).reshape(*lead, MLA_HEADS * LANES)
    wv = w[..., MLA_NOPE_DIM:].reshape(*lead, MLA_HEADS * MLA_V_DIM)
    return wk, wv


def _mla_krope_layout(w):
    lead = w.shape[:-1]
    kr = w.reshape(*lead, 2, 16)
    return jnp.concatenate([jnp.zeros((*lead, 2, 32), w.dtype), kr,
                            jnp.zeros((*lead, 2, 16), w.dtype)], axis=-1).reshape(*lead, LANES)


def _in_proj_layout(w):
    sl = lambda off, n: w[..., off:off + n]
    return jnp.concatenate([
        _pair_layout(sl(_OFF_DQ, 512)), _pair_layout(sl(_OFF_DK, 512)),
        sl(_OFF_CQ, MLA_Q_RANK), sl(_OFF_CKV, MLA_KV_RANK),
        _mla_krope_layout(sl(_OFF_KR, MLA_ROPE_DIM)),
        _pair_layout(sl(_OFF_RQ, 256)), _pair_layout(sl(_OFF_RK, 256))], axis=-1)


def _in_proj_layout_t(w):
    return jnp.concatenate([w[..., _OFF_DV:_OFF_DV + 512], w[..., _OFF_RV:_OFF_RV + 256],
                            w[..., _OFF_RG:_OFF_RG + 256]], axis=-1).swapaxes(-1, -2)


_P_DQ, _P_DK = 0, 512
_P_CQ, _P_CKV, _P_KR = 1024, 1280, 1408
_P_RQ, _P_RK = 1536, 1792
_P_COLS = 2048
_T_DV, _T_RV, _T_RG = 0, 512, 768
_T_ROWS = 1024


def _rope_lane_freqs():
    f64 = 1.0 / (ROPE_THETA ** (jnp.arange(0, 64, 2, dtype=F32) / 64))
    f32 = 1.0 / (ROPE_THETA ** (jnp.arange(0, 32, 2, dtype=F32) / 32))
    half = jnp.concatenate([f64, f32, jnp.zeros((HALF - 48,), F32)])
    return jnp.concatenate([half, half])[None, :]


def _rope_table_kernel(pos_ref, invf_ref, c64_ref, s64_ref, cm_ref, sm_ref):
    half = pos_ref.shape[0] // 2
    lane = lax.broadcasted_iota(jnp.int32, (1, LANES), 1)
    a = jnp.where(lane < HALF, pos_ref[0:half, :], pos_ref[half:2 * half, :]) * invf_ref[...]
    c2, s2 = jnp.cos(a), jnp.sin(a)
    sign = jnp.where(lane < HALF, -1.0, 1.0)

    def spread(t, width, period):
        out = t
        for shift in range(period, LANES, period):
            out = jnp.where((lane >= shift) & (lane < shift + width), pltpu.roll(t, shift, 1), out)
        return out

    rope = ((lane % HALF) >= 32) & ((lane % HALF) < 48)
    for part in range(2):
        rows = slice(part * half, (part + 1) * half)
        c = c2 if part == 0 else pltpu.roll(c2, HALF, 1)
        s = s2 if part == 0 else pltpu.roll(s2, HALF, 1)
        c64_ref[rows, :] = spread(c, 32, 32)
        s64_ref[rows, :] = spread(s, 32, 32) * sign
        cm_ref[rows, :] = jnp.where(rope, spread(c, 48, HALF), 1.0)
        sm_ref[rows, :] = jnp.where(rope, spread(s, 48, HALF) * sign, 0.0)


def _rope_tables(pos, invf):
    m = pos.shape[0]
    bm = 2048
    tab = jax.ShapeDtypeStruct((m, LANES), F32)
    row = pl.BlockSpec((bm, LANES), lambda i: (i, 0))
    return pl.pallas_call(
        _rope_table_kernel,
        grid=(m // bm,),
        in_specs=[pl.BlockSpec((bm, 1), lambda i: (i, 0)),
                  pl.BlockSpec((1, LANES), lambda i: (0, 0))],
        out_specs=[row, row, row, row],
        out_shape=[tab, tab, tab, tab],
        name="rope_tables",
    )(pos, invf)


def _rms(t, gain, eps):
    return t * lax.rsqrt(jnp.mean(t * t, axis=-1, keepdims=True) + eps) * gain


def _rope_blocks(y, cos, sin):
    outs = []
    for j in range(y.shape[1] // LANES):
        blk = y[:, j * LANES:(j + 1) * LANES]
        outs.append(blk * cos + pltpu.roll(blk, HALF, 1) * sin)
    return outs[0] if len(outs) == 1 else jnp.concatenate(outs, axis=1)


def _pre_kernel(x_ref, g_ref, w_ref, wt_ref, c64_ref, s64_ref, cm_ref, sm_ref,
                qn_ref, wuq_ref, kvn_ref, wuk_ref, wuvt_ref,
                dqt_ref, dk_ref, dvt_ref, mqt_ref, mk_ref, mvt_ref,
                rq_ref, rk_ref, rvt_ref, rgt_ref, *, diff_qscale, mla_qscale, ret_kscale):
    hb = _rms(x_ref[...], g_ref[...], NORM_EPS).astype(BF16)

    def proj(off, n):
        return jnp.dot(hb, w_ref[:, off:off + n], preferred_element_type=F32)

    c64, s64 = c64_ref[...], s64_ref[...]
    cm, sm = cm_ref[...], sm_ref[...]

    def store_t(ref, yt):
        for tb in range(ref.shape[0]):
            ref[tb] = yt[:, tb * TOKEN_BLOCK:(tb + 1) * TOKEN_BLOCK].astype(ref.dtype)

    dqt = (_rope_blocks(proj(_P_DQ, 512), c64, s64) * diff_qscale).T
    feat = lax.broadcasted_iota(jnp.int32, (LANES, 1), 0)
    copies = []
    for h in range(DIFF_HEADS):
        head = dqt[h * LANES:(h + 1) * LANES]
        copies += [jnp.where(((feat % HALF) // 32) == t, head, 0.0) for t in range(2)]
    store_t(dqt_ref, jnp.concatenate(copies, axis=0))
    dk_ref[...] = _rope_blocks(proj(_P_DK, 512), c64, s64).astype(BF16)
    yt = lax.dot_general(wt_ref[...], hb, _NT, preferred_element_type=F32)
    store_t(dvt_ref, yt[_T_DV:_T_DV + 512])
    store_t(rvt_ref, yt[_T_RV:_T_RV + 256])
    store_t(rgt_ref, yt[_T_RG:_T_RG + 256])

    cq = _rms(proj(_P_CQ, MLA_Q_RANK), qn_ref[...], NORM_EPS).astype(BF16)
    q = jnp.dot(cq, wuq_ref[...], preferred_element_type=F32)
    store_t(mqt_ref, (_rope_blocks(q, cm, sm) * mla_qscale).T)
    ckv = _rms(proj(_P_CKV, MLA_KV_RANK), kvn_ref[...], NORM_EPS).astype(BF16)
    kn = jnp.dot(ckv, wuk_ref[...], preferred_element_type=F32)
    kr = _rope_blocks(proj(_P_KR, LANES), cm, sm)
    mk_ref[...] = (kn + jnp.concatenate([kr] * MLA_HEADS, axis=1)).astype(BF16)
    store_t(mvt_ref, lax.dot_general(wuvt_ref[...], ckv, _NT,
                                     preferred_element_type=F32))

    rq_ref[...] = _rope_blocks(proj(_P_RQ, 256), c64, s64).astype(BF16)
    rk_ref[...] = (_rope_blocks(proj(_P_RK, 256), c64, s64) * ret_kscale).astype(BF16)


def _layer_spec(shape, layer):
    return pl.BlockSpec((None, *shape), lambda *_: (layer,) + (0,) * len(shape),
                        pipeline_mode=pl.Buffered(1))


def _pre_attention(layer, x, gain, w_all, wt_all, tabs, qn, wuq, kvn, wuk, wuvt):
    _const_spec = functools.partial(_layer_spec, layer=layer)
    m = x.shape[0]
    bm = PRE_BM
    nb = m // bm
    row = lambda n: pl.BlockSpec((bm, n), lambda i: (i, 0))
    tpb = bm // TOKEN_BLOCK
    tr = lambda n: pl.BlockSpec((tpb, n, TOKEN_BLOCK), lambda i: (i, 0, 0))
    out = lambda n, dt=BF16: jax.ShapeDtypeStruct((m, n), dt)
    out_t = lambda n, dt=BF16: jax.ShapeDtypeStruct((m // TOKEN_BLOCK, n, TOKEN_BLOCK), dt)
    kern = functools.partial(
        _pre_kernel,
        diff_qscale=DIFF_HEAD_DIM ** -0.5 * LOG2E,
        mla_qscale=(MLA_NOPE_DIM + MLA_ROPE_DIM) ** -0.5 * LOG2E,
        ret_kscale=RET_QK_DIM ** -0.5)
    return pl.pallas_call(
        kern,
        grid=(nb,),
        in_specs=[row(D_MODEL), _const_spec((1, D_MODEL)), _const_spec((D_MODEL, _P_COLS)),
                  _const_spec((_T_ROWS, D_MODEL)),
                  row(LANES), row(LANES), row(LANES), row(LANES),
                  _const_spec((1, MLA_Q_RANK)), _const_spec((MLA_Q_RANK, 512)),
                  _const_spec((1, MLA_KV_RANK)), _const_spec((MLA_KV_RANK, 512)),
                  _const_spec((256, MLA_KV_RANK))],
        out_specs=[tr(1024), row(512), tr(512), tr(512), row(512), tr(256),
                   row(256), row(256), tr(256), tr(256)],
        out_shape=[out_t(1024), out(512), out_t(512), out_t(512), out(512), out_t(256),
                   out(256), out(256), out_t(256), out_t(256, F32)],
        compiler_params=pltpu.CompilerParams(
            dimension_semantics=("parallel",), vmem_limit_bytes=VMEM_LIMIT),
        name="pre_attention",
    )(x, gain, w_all, wt_all, *tabs, qn, wuq, kvn, wuk, wuvt)


def _flash_kernel(*refs, diff, blk, sub, nblk, lam_init):
    if diff:
        qt_ref, k_ref, vt_ref, lam_ref, subln_ref, o_ref, s_sc, mx_sc, m_sc, acc_sc = refs
    else:
        qt_ref, k_ref, vt_ref, o_ref, s_sc, mx_sc, m_sc, acc_sc = refs
    nsub = blk // sub
    entry = 2

    q_rows = [slice(0, LANES), slice(LANES, 2 * LANES)]
    if diff:
        k_lanes = [slice(0, LANES)] * 2
        v_rows = [slice(0, LANES)] * 2
    else:
        k_lanes = [slice(0, LANES), slice(LANES, 2 * LANES)]
        v_rows = [slice(0, MLA_V_DIM), slice(MLA_V_DIM, 2 * MLA_V_DIM)]

    def scores(i, j, slot, diagonal=False):
        start = pl.multiple_of(j * blk, blk)
        for t in range(2):
            for c in range(nsub):
                cols = slice(c * sub, (c + 1) * sub)
                nk = (c + 1) * sub if diagonal else blk
                kj = k_ref[pl.ds(start, nk), k_lanes[t]]
                qt = qt_ref[i, q_rows[t], cols]
                s = jnp.dot(kj, qt, preferred_element_type=F32)
                s_sc[slot, t, c, 0:nk, :] = s
                if not diagonal:
                    mx_sc[slot, t, :, cols] = jnp.max(s, axis=0, keepdims=True)

    def accumulate(j, slot, diagonal):
        for t in range(2):
            for c in range(nsub):
                cols = slice(c * sub, (c + 1) * sub)
                nk = (c + 1) * sub if diagonal else blk
                s = s_sc[slot, t, c, 0:nk, :]
                if diagonal:
                    key = lax.broadcasted_iota(jnp.int32, (nk, sub), 0)
                    qry = lax.broadcasted_iota(jnp.int32, (nk, sub), 1) + c * sub
                    s = jnp.where(key <= qry, s, -jnp.inf)
                    m_cur = jnp.max(s, axis=0, keepdims=True)
                else:
                    m_cur = mx_sc[slot, t, :, cols]
                m_prev = m_sc[t, :, cols]
                m_new = jnp.maximum(m_prev, m_cur)
                alpha = jnp.exp2(m_prev - m_new)
                p = jnp.exp2(s - m_new)
                vt = jnp.concatenate([vt_ref[j, v_rows[t], 0:nk],
                                      jnp.ones((SUM_ROWS, nk), BF16)], axis=0)
                acc_sc[t, c] = alpha * acc_sc[t, c] + jnp.dot(
                    vt, p.astype(BF16), preferred_element_type=F32)
                m_sc[t, :, cols] = m_new

    if diff:
        lv = lam_ref[...]
        lam = (jnp.exp(jnp.sum(lv[0:1] * lv[1:2], axis=1, keepdims=True))
               - jnp.exp(jnp.sum(lv[2:3] * lv[3:4], axis=1, keepdims=True)) + lam_init)

    def finalize(i):
        for c in range(nsub):
            cols = slice(c * sub, (c + 1) * sub)
            dv = acc_sc.shape[2] - SUM_ROWS
            o0 = acc_sc[0, c, 0:dv, :] * (1.0 / acc_sc[0, c, dv:dv + 1, :])
            o1 = acc_sc[1, c, 0:dv, :] * (1.0 / acc_sc[1, c, dv:dv + 1, :])
            if diff:
                o = o0 - lam * o1
                ms = jnp.mean(o * o, axis=0, keepdims=True)
                o = o * lax.rsqrt(ms + DIFF_SUBLN_EPS) * subln_ref[:, cols] * (1.0 - lam_init)
            else:
                o = jnp.concatenate([o0, o1], axis=0)
            row = pl.multiple_of(i * blk + c * sub, sub)
            o_ref[pl.ds(row, sub), :] = o.T.astype(o_ref.dtype)

    scores(0, 0, entry, diagonal=True)

    def query_block(i, carry):
        i_next = jnp.minimum(i + 1, nblk - 1)
        m_sc[...] = jnp.full(m_sc.shape, -jnp.inf, F32)
        acc_sc[...] = jnp.zeros(acc_sc.shape, F32)

        @pl.when(i == 0)
        def _():
            accumulate(i, entry, True)
            scores(i_next, i_next, entry, diagonal=True)

        @pl.when(i > 0)
        def _():
            scores(i, 0, 0)
            accumulate(i, entry, True)

        def pair(jj, c2):
            j0 = 2 * jj
            scores(i, j0 + 1, 1)
            accumulate(j0, 0, False)
            scores(i, j0 + 2, 0)
            accumulate(j0 + 1, 1, False)
            return c2

        npairs = jnp.maximum(i - 1, 0) // 2

        def quad(qq, c2):
            pair(2 * qq, c2)
            return pair(2 * qq + 1, c2)

        lax.fori_loop(0, npairs // 2, quad, 0)

        @pl.when(npairs % 2 == 1)
        def _():
            pair(npairs - 1, 0)

        @pl.when(i % 2 == 1)
        def _():
            scores(i_next, i_next, entry, diagonal=True)
            accumulate(i - 1, 0, False)

        @pl.when(jnp.logical_and(i % 2 == 0, i > 0))
        def _():
            scores(i, i - 1, 1)
            accumulate(i - 2, 0, False)
            scores(i_next, i_next, entry, diagonal=True)
            accumulate(i - 1, 1, False)

        finalize(i)
        return carry

    lax.fori_loop(0, nblk, query_block, 0)


def _flash_attention(qt, k, vt, diff, layer=0, lam=None, subln=None, lam_init=0.0):
    b, s, _ = k.shape
    blk, sub = ATT_BLOCK, ATT_SUB
    nblk, nsub = s // blk, blk // sub
    width = LANES if diff else 2 * LANES
    groups = k.shape[2] // width
    dv = LANES if diff else MLA_V_DIM
    qspec = pl.BlockSpec((None, nblk, 2 * LANES, blk), lambda bi, g: (bi, 0, g, 0))
    kspec = pl.BlockSpec((None, s, width), lambda bi, g: (bi, 0, g))
    vspec = pl.BlockSpec((None, nblk, LANES, blk), lambda bi, g: (bi, 0, g, 0))
    ospec = pl.BlockSpec((None, s, LANES), lambda bi, g: (bi, 0, g))
    in_specs = [qspec, kspec, vspec]
    args = [qt, k, vt]
    if diff:
        in_specs += [_layer_spec((4, DIFF_HEAD_DIM), layer), _layer_spec((LANES, blk), layer)]
        args += [lam, subln]
    return pl.pallas_call(
        functools.partial(_flash_kernel, diff=diff, blk=blk, sub=sub, nblk=nblk,
                          lam_init=lam_init),
        grid=(b, groups),
        in_specs=in_specs,
        out_specs=ospec,
        out_shape=jax.ShapeDtypeStruct((b, s, groups * LANES), BF16),
        scratch_shapes=[pltpu.VMEM((3, 2, nsub, blk, sub), F32),
                        pltpu.VMEM((3, 2, 1, blk), F32),
                        pltpu.VMEM((2, 1, blk), F32),
                        pltpu.VMEM((2, nsub, dv + SUM_ROWS, sub), F32)],
        compiler_params=pltpu.CompilerParams(
            dimension_semantics=("parallel", "parallel"),
            vmem_limit_bytes=VMEM_LIMIT),
        name="diff_attention" if diff else "mla_attention",
    )(*args)


def _retention_consts():
    h, c = RET_HEADS, RET_CHUNK
    f32 = np.float32
    log_gamma = np.log1p(-np.exp2(f32(-5.0) - np.arange(h, dtype=f32))).astype(f32)
    pos = np.arange(c, dtype=f32)
    rel = pos[:, None] - pos[None, :]
    decay = np.where(rel[None] >= 0,
                     np.exp(np.maximum(rel, f32(0.0))[None] * log_gamma[:, None, None]), f32(0.0))
    xi = np.exp((pos[None, :] + f32(1.0)) * log_gamma[:, None])
    zeta = np.exp((f32(c - 1.0) - pos)[None, :] * log_gamma[:, None])
    cdec = np.exp(f32(c) * log_gamma)
    decay_t = np.swapaxes(decay, 1, 2).astype(f32)
    xi_row = xi[:, None, :].astype(f32)
    zeta_col = np.broadcast_to(zeta[:, :, None], (h, c, LANES)).astype(f32)
    cdec_b = np.broadcast_to(cdec[:, None, None], (h, 1, LANES)).astype(f32)
    return tuple(jnp.asarray(t) for t in (decay_t, xi_row, zeta_col, cdec_b))


def _retention_kernel(q_ref, k_ref, vt_ref, gt_ref, dec_ref, xi_ref, zeta_ref, cdec_ref,
                      gw_ref, gb_ref, o_ref, st_sc, *, rows):
    @pl.when(pl.program_id(1) == 0)
    def _():
        st_sc[...] = jnp.zeros(st_sc.shape, F32)

    lane = lax.broadcasted_iota(jnp.int32, (1, LANES), 1)
    pair_sel = [((lane % HALF) // 32) == hh for hh in range(2)]
    c = RET_CHUNK
    dv = RET_V_DIM
    for ci in range(rows // c):
        tok = slice(ci * c, (ci + 1) * c)
        tb = (ci * c) // TOKEN_BLOCK
        tok_t = slice(ci * c - tb * TOKEN_BLOCK, (ci + 1) * c - tb * TOKEN_BLOCK)
        for pv in range(RET_HEADS // 2):
            lanes = slice(pv * LANES, (pv + 1) * LANES)
            qp = q_ref[tok, lanes]
            kp = k_ref[tok, lanes]
            vtp = vt_ref[tb, lanes, tok_t]
            halves = []
            for hh in range(2):
                h = 2 * pv + hh
                qh = jnp.where(pair_sel[hh], qp, jnp.zeros_like(qp))
                kh = jnp.where(pair_sel[hh], kp, jnp.zeros_like(kp))
                st = lax.dot_general(kh, qh, _NT, preferred_element_type=F32)
                inner = jnp.dot(vtp, (st * dec_ref[h]).astype(BF16),
                                preferred_element_type=F32)
                state = st_sc[h]
                cross = lax.dot_general(state.astype(BF16), qh, _NT,
                                        preferred_element_type=F32) * xi_ref[h]
                kz = (kh.astype(F32) * zeta_ref[h]).astype(BF16)
                st_sc[h] = (jnp.dot(vtp, kz, preferred_element_type=F32)
                            + cdec_ref[h] * state)
                o = (inner + cross)[hh * dv:(hh + 1) * dv]
                mu = jnp.mean(o, axis=0, keepdims=True)
                d = o - mu
                var = jnp.mean(d * d, axis=0, keepdims=True)
                halves.append(d * lax.rsqrt(var + RET_GN_EPS))
            on = jnp.concatenate(halves, axis=0) * gw_ref[lanes, :] + gb_ref[lanes, :]
            g = gt_ref[tb, lanes, tok_t]
            o_ref[tok, lanes] = (g * jax.nn.sigmoid(g) * on).T.astype(o_ref.dtype)


def _retention(layer, q, k, vt, gt, consts, gw, gb):
    b, s, w = q.shape
    rows = RET_ROWS
    c = RET_CHUNK
    blk = pl.BlockSpec((None, rows, w), lambda bi, i: (bi, i, 0))
    blk_t = pl.BlockSpec((None, rows // TOKEN_BLOCK, w, TOKEN_BLOCK), lambda bi, i: (bi, i, 0, 0))
    whole = lambda *shape: pl.BlockSpec(shape, lambda bi, i: (0,) * len(shape))
    return pl.pallas_call(
        functools.partial(_retention_kernel, rows=rows),
        grid=(b, s // rows),
        in_specs=[blk, blk, blk_t, blk_t,
                  whole(RET_HEADS, c, c), whole(RET_HEADS, 1, c),
                  whole(RET_HEADS, c, LANES), whole(RET_HEADS, 1, LANES),
                  _layer_spec((w, c), layer), _layer_spec((w, c), layer)],
        out_specs=blk,
        out_shape=jax.ShapeDtypeStruct((b, s, w), BF16),
        scratch_shapes=[pltpu.VMEM((RET_HEADS, LANES, LANES), F32)],
        compiler_params=pltpu.CompilerParams(
            dimension_semantics=("parallel", "arbitrary")),
        name="retention",
    )(q, k, vt, gt, *consts, gw, gb)


def _post_kernel(x_ref, od_ref, om_ref, or_ref, wo_ref, g_ref,
                 wg_ref, wu_ref, wd_ref, fg_ref, o_ref, a_sc, *, final):
    x1 = (x_ref[...]
          + jnp.dot(od_ref[...], wo_ref[0:512, :], preferred_element_type=F32)
          + jnp.dot(om_ref[...], wo_ref[512:768, :], preferred_element_type=F32)
          + jnp.dot(or_ref[...], wo_ref[768:1024, :], preferred_element_type=F32))
    hb = _rms(x1, g_ref[...], NORM_EPS).astype(BF16)
    for c0 in range(0, FFN_HIDDEN, FFN_CHUNK):
        gate = jnp.dot(hb, wg_ref[:, c0:c0 + FFN_CHUNK], preferred_element_type=F32)
        up = jnp.dot(hb, wu_ref[:, c0:c0 + FFN_CHUNK], preferred_element_type=F32)
        a_sc[:, c0:c0 + FFN_CHUNK] = (gate * jax.nn.sigmoid(gate) * up).astype(BF16)
    y = x1 + jnp.dot(a_sc[...], wd_ref[...], preferred_element_type=F32)
    if final:
        y = _rms(y, fg_ref[...], NORM_EPS)
    o_ref[...] = y


def _post_attention(layer, x, od, om, orr, wo, gain, wg, wu, wd, fgain, final):
    _const_spec = functools.partial(_layer_spec, layer=layer)
    m = x.shape[0]
    bm = POST_BM
    row = lambda n: pl.BlockSpec((bm, n), lambda i: (i, 0))
    return pl.pallas_call(
        functools.partial(_post_kernel, final=final),
        grid=(m // bm,),
        in_specs=[row(D_MODEL), row(512), row(256), row(256),
                  _const_spec((D_MODEL, D_MODEL)), _const_spec((1, D_MODEL)),
                  _const_spec((D_MODEL, FFN_HIDDEN)), _const_spec((D_MODEL, FFN_HIDDEN)),
                  _const_spec((FFN_HIDDEN, D_MODEL)),
                  pl.BlockSpec((1, D_MODEL), lambda i: (0, 0))],
        out_specs=row(D_MODEL),
        out_shape=jax.ShapeDtypeStruct((m, D_MODEL), F32),
        scratch_shapes=[pltpu.VMEM((bm, FFN_HIDDEN), BF16)],
        compiler_params=pltpu.CompilerParams(
            dimension_semantics=("parallel",), vmem_limit_bytes=VMEM_LIMIT),
        name="post_attention",
    )(x, od, om, orr, wo, gain, wg, wu, wd, fgain)


def kernel(x, positions, attn_norm, w_in, diff_lam_q1, diff_lam_k1, diff_lam_q2, diff_lam_k2, diff_subln, mla_q_norm, mla_w_uq, mla_kv_norm, mla_w_ukv, ret_gn_w, ret_gn_b, w_out, ffn_norm, w_gate, w_up, w_down, final_norm):
    b, s, d = x.shape
    m = b * s
    nblk = s // ATT_BLOCK
    tabs = _rope_tables(positions.astype(F32).reshape(m, 1), _rope_lane_freqs())
    ret_consts = _retention_consts()

    row_vec = lambda t: t[:, None, :]
    w_all = _in_proj_layout(w_in).astype(BF16)
    wt_all = _in_proj_layout_t(w_in).astype(BF16)
    wuq = _mla_q_layout(mla_w_uq).astype(BF16)
    wuk, wuv = _mla_kv_layout(mla_w_ukv)
    wuk, wuvt = wuk.astype(BF16), wuv.swapaxes(-1, -2).astype(BF16)
    lam = jnp.stack([diff_lam_q1, diff_lam_k1, diff_lam_q2, diff_lam_k2], axis=1)
    subln = jnp.broadcast_to(diff_subln[:, :, None], (DEPTH, LANES, ATT_BLOCK))
    gn_col = lambda t: jnp.broadcast_to(t[:, :, None], (*t.shape, RET_CHUNK))
    gn_w, gn_b = gn_col(ret_gn_w), gn_col(ret_gn_b)
    wo, wg, wu, wd = (t.astype(BF16) for t in (w_out, w_gate, w_up, w_down))
    attn_g, ffn_g, qn_g, kvn_g = (row_vec(t) for t in (attn_norm, ffn_norm, mla_q_norm, mla_kv_norm))

    sh = lambda t: t.reshape(b, s, t.shape[-1])
    sh_t = lambda t: t.reshape(b, nblk, t.shape[1], ATT_BLOCK)
    xf = x.reshape(m, d)
    for layer in range(DEPTH):
        lam_init = 0.8 - 0.6 * math.exp(-0.3 * layer)
        (dqt, dk, dvt, mqt, mk, mvt, rq, rk, rvt, rgt) = _pre_attention(
            layer, xf, attn_g, w_all, wt_all, tabs, qn_g, wuq, kvn_g, wuk, wuvt)
        o_diff = _flash_attention(sh_t(dqt), sh(dk), sh_t(dvt), True, layer=layer, lam=lam,
                                  subln=subln, lam_init=lam_init)
        o_mla = _flash_attention(sh_t(mqt), sh(mk), sh_t(mvt), False)
        o_ret = _retention(layer, sh(rq), sh(rk), sh_t(rvt), sh_t(rgt), ret_consts, gn_w, gn_b)
        xf = _post_attention(
            layer, xf, o_diff.reshape(m, 512), o_mla.reshape(m, 256), o_ret.reshape(m, 256),
            wo, ffn_g, wg, wu, wd, final_norm[None], final=(layer == DEPTH - 1))
    return xf.reshape(b, s, d)
```
